```python
import jax, jax.numpy as jnp
from jax import lax
import numpy as np

D_MODEL = 2048
BATCH = 2
SEQ = 4096
DEPTH = 1

CONV_WIDTH = D_MODEL // 2
CONV_GROUPS = 8
CONV_GROUP_DIM = CONV_WIDTH // CONV_GROUPS
CONV_KERNEL = 31
HGRN_WIDTH = D_MODEL - CONV_WIDTH
HGRN_HEADS = 8
HGRN_EXPAND = HGRN_WIDTH // HGRN_HEADS
HGRN_HEAD_DIM = HGRN_WIDTH // HGRN_HEADS
HGRN_CHUNK = 64
HGRN_SUB = 8
IN_PROJ_DIM = 2 * CONV_WIDTH + 4 * HGRN_WIDTH
D_FF = 5632
FFN_KERNEL = 3
LN_EPS = 1e-5
RMS_EPS = 1e-6
ALPHA = (2.0 * DEPTH) ** 0.25
BETA = (8.0 * DEPTH) ** -0.25

kernel_name = "hymba_conformer_hgrn2_deepnorm"


def layer_norm(x, g, b):
    xf = x.astype(jnp.float32)
    mu = jnp.mean(xf, axis=-1, keepdims=True)
    var = jnp.mean(jnp.square(xf - mu), axis=-1, keepdims=True)
    y = (xf - mu) * lax.rsqrt(var + LN_EPS)
    return (y * g.astype(jnp.float32) + b.astype(jnp.float32)).astype(x.dtype)


def causal_dwconv(x, w, b):
    k = w.shape[0]
    c = x.shape[-1]
    y = lax.conv_general_dilated(
        x, w[:, None, :].astype(x.dtype), window_strides=(1,), padding=[(k - 1, 0)],
        dimension_numbers=("NWC", "WIO", "NWC"), feature_group_count=c)
    return y + b.astype(x.dtype)


def hgrn2_recurrence(q, k, v, log_f):
    B, T, H, N = q.shape
    Dv = v.shape[-1]
    C, L = HGRN_CHUNK, HGRN_SUB
    nc, ns = T // C, C // L

    def blocks(t):
        return t.astype(jnp.float32).reshape(B, nc, ns, L, H, -1).transpose(0, 4, 1, 2, 3, 5)

    qb, kb, vb = blocks(q), blocks(k), blocks(v)
    gb = blocks(log_f)
    cum = jnp.cumsum(gb.reshape(B, H, nc, C, N), axis=3)
    bb = cum.reshape(B, H, nc, ns, L, N)
    bend = bb[..., -1, :]

    idx = jnp.arange(ns)
    blk_lower = idx[:, None] > idx[None, :]
    e_off = bb[:, :, :, :, None, :, :] - bend[:, :, :, None, :, None, :]
    q_off = qb[:, :, :, :, None] * jnp.exp(jnp.where(blk_lower[:, :, None, None], e_off, -jnp.inf))
    k_off = kb * jnp.exp(bend[..., None, :] - bb)
    a_off = jnp.einsum("bhcijtn,bhcjsn->bhcijts", q_off, k_off)
    o_off = jnp.einsum("bhcijts,bhcjsd->bhcitd", a_off, vb)

    pos = jnp.arange(L)
    causal = pos[:, None] >= pos[None, :]
    e_d = bb[..., :, None, :] - bb[..., None, :, :]
    decay = jnp.exp(jnp.where(causal[:, :, None], e_d, -jnp.inf))
    a_d = jnp.einsum("bhcitn,bhcitsn,bhcisn->bhcits", qb, decay, kb)
    o_d = jnp.einsum("bhcits,bhcisd->bhcitd", a_d, vb)
    o_intra = (o_off + o_d).reshape(B, H, nc, C, Dv)

    qc = qb.reshape(B, H, nc, C, N)
    kc = kb.reshape(B, H, nc, C, N)
    vc = vb.reshape(B, H, nc, C, Dv)
    blast = cum[..., -1, :]
    q_in = qc * jnp.exp(cum)
    k_up = kc * jnp.exp(blast[..., None, :] - cum)

    def step(S, xs):
        q_i, k_i, v_i, bl_i = xs
        o_i = jnp.einsum("bhtn,bhnd->bhtd", q_i, S)
        S = S * jnp.exp(bl_i)[..., None] + jnp.einsum("bhtn,bhtd->bhnd", k_i, v_i)
        return S, o_i

    xs = (jnp.moveaxis(q_in, 2, 0), jnp.moveaxis(k_up, 2, 0),
          jnp.moveaxis(vc, 2, 0), jnp.moveaxis(blast, 2, 0))
    S0 = jnp.zeros((B, H, N, Dv), jnp.float32)
    _, o_inter = lax.scan(step, S0, xs)
    o = o_intra + jnp.moveaxis(o_inter, 0, 2)
    return o.reshape(B, H, T, Dv).transpose(0, 2, 1, 3)


def hybrid_mixer(x, w_in, conv_w, conv_b, conv_norm_g, conv_norm_b, lb_logits, hgrn_norm_g, w_out, layer):
    B, T, _ = x.shape
    h = x @ w_in
    c1, c2 = CONV_WIDTH, 2 * CONV_WIDTH
    a, gate = h[..., :c1], h[..., c1:c2]
    q, f, i, og = (h[..., c2 + n * HGRN_WIDTH: c2 + (n + 1) * HGRN_WIDTH] for n in range(4))

    u = a * jax.nn.sigmoid(gate)
    u = causal_dwconv(u, conv_w, conv_b)
    u = layer_norm(u.reshape(B, T, CONV_GROUPS, CONV_GROUP_DIM),
                   conv_norm_g.reshape(CONV_GROUPS, CONV_GROUP_DIM),
                   conv_norm_b.reshape(CONV_GROUPS, CONV_GROUP_DIM)).reshape(B, T, CONV_WIDTH)
    u = jax.nn.silu(u)

    lb_table = jnp.cumsum(jax.nn.softmax(lb_logits.astype(jnp.float32), axis=0), axis=0)
    lb = lb_table[layer]
    fg = lb + (1.0 - lb) * jax.nn.sigmoid(f.astype(jnp.float32))
    log_f = jnp.log(fg)
    kk = 1.0 - fg
    qh = jax.nn.silu(q.astype(jnp.float32))
    shp = (B, T, HGRN_HEADS, HGRN_EXPAND)
    o = hgrn2_recurrence(qh.reshape(shp), kk.reshape(shp),
                         i.reshape(B, T, HGRN_HEADS, HGRN_HEAD_DIM), log_f.reshape(shp))
    o = o * lax.rsqrt(jnp.mean(jnp.square(o), axis=-1, keepdims=True) + RMS_EPS)
    o = o.reshape(B, T, HGRN_WIDTH) * hgrn_norm_g.astype(jnp.float32)
    o = (o * jax.nn.silu(og.astype(jnp.float32))).astype(x.dtype)

    return jnp.concatenate([u, o], axis=-1) @ w_out


def conv_ffn(x, w_up, conv_w, conv_b, w_down):
    h = x @ w_up
    g, v = h[..., :D_FF], h[..., D_FF:]
    g = causal_dwconv(g, conv_w, conv_b)
    return (jax.nn.silu(g) * v) @ w_down


def setup_inputs(seed: int = 0) -> dict:
    key = jax.random.key(seed)
    ks = jax.random.split(key, 20)
    f32 = jnp.float32
    nrm = lambda k, s: jax.random.normal(k, s, f32)
    return {
        "x": nrm(ks[0], (BATCH, SEQ, D_MODEL)),
        "emb_ln_g": 1.0 + 0.02 * nrm(ks[1], (D_MODEL,)),
        "emb_ln_b": 0.02 * nrm(ks[2], (D_MODEL,)),
        "w_in": nrm(ks[3], (DEPTH, D_MODEL, IN_PROJ_DIM)) * D_MODEL ** -0.5,
        "conv_w": nrm(ks[4], (DEPTH, CONV_KERNEL, CONV_WIDTH)) * CONV_KERNEL ** -0.5,
        "conv_b": 0.01 * nrm(ks[5], (DEPTH, CONV_WIDTH)),
        "conv_norm_g": 1.0 + 0.02 * nrm(ks[6], (DEPTH, CONV_WIDTH)),
        "conv_norm_b": 0.02 * nrm(ks[7], (DEPTH, CONV_WIDTH)),
        "lb_logits": 0.5 * nrm(ks[8], (DEPTH + 1, HGRN_WIDTH)),
        "hgrn_norm_g": 1.0 + 0.02 * nrm(ks[9], (DEPTH, HGRN_WIDTH)),
        "w_out": nrm(ks[10], (DEPTH, D_MODEL, D_MODEL)) * (D_MODEL ** -0.5) * BETA,
        "ln1_g": 1.0 + 0.02 * nrm(ks[11], (DEPTH, D_MODEL)),
        "ln1_b": 0.02 * nrm(ks[12], (DEPTH, D_MODEL)),
        "w_ffn_up": nrm(ks[13], (DEPTH, D_MODEL, 2 * D_FF)) * D_MODEL ** -0.5,
        "ffn_conv_w": nrm(ks[14], (DEPTH, FFN_KERNEL, D_FF)) * FFN_KERNEL ** -0.5,
        "ffn_conv_b": 0.01 * nrm(ks[15], (DEPTH, D_FF)),
        "w_ffn_down": nrm(ks[16], (DEPTH, D_FF, D_MODEL)) * (D_FF ** -0.5) * BETA,
        "ln2_g": 1.0 + 0.02 * nrm(ks[17], (DEPTH, D_MODEL)),
        "ln2_b": 0.02 * nrm(ks[18], (DEPTH, D_MODEL)),
    }


def reference(x, emb_ln_g, emb_ln_b, w_in, conv_w, conv_b, conv_norm_g, conv_norm_b, lb_logits,
              hgrn_norm_g, w_out, ln1_g, ln1_b, w_ffn_up, ffn_conv_w, ffn_conv_b, w_ffn_down,
              ln2_g, ln2_b):
    h = layer_norm(x, emb_ln_g, emb_ln_b)
    for l in range(DEPTH):
        mix = hybrid_mixer(h, w_in[l], conv_w[l], conv_b[l], conv_norm_g[l], conv_norm_b[l],
                           lb_logits, hgrn_norm_g[l], w_out[l], l)
        h = layer_norm(ALPHA * h + mix, ln1_g[l], ln1_b[l])
        ffn = conv_ffn(h, w_ffn_up[l], ffn_conv_w[l], ffn_conv_b[l], w_ffn_down[l])
        h = layer_norm(ALPHA * h + ffn, ln2_g[l], ln2_b[l])
    return h
```

```python
import functools

import jax
import jax.numpy as jnp
import numpy as np
from jax import lax
from jax.experimental import pallas as pl
from jax.experimental.pallas import tpu as pltpu

F32 = jnp.float32
BF16 = jnp.bfloat16

D_MODEL = 2048
CONV_WIDTH = 1024
CONV_GROUPS = 8
CONV_KERNEL = 31
HGRN_WIDTH = 1024
HGRN_HEADS = 8
HEAD_DIM = 128
IN_PROJ_DIM = 2 * CONV_WIDTH + 4 * HGRN_WIDTH
D_FF = 5632
FFN_KERNEL = 3
LN_EPS = 1e-5
RMS_EPS = 1e-6
DEPTH = 1
ALPHA = (2.0 * DEPTH) ** 0.25

LANES = 128
SUBLANES = 8
VMEM_LIMIT = 56 * 1024 * 1024

TM_IN, TN_IN = 1024, 1024
TT_CONV = 256
RS_CONV = 64
CONV_HALO = 32
TT_HGRN = 512
CHUNK = 64
N_LEVELS = 6
TM_OUT = 512
TM_FFN, TF_FFN = 512, 512
LN_ROWS = 64


def _layer_norm(x, g, b, eps):
    mu = jnp.mean(x, axis=-1, keepdims=True)
    xc = x - mu
    var = jnp.mean(xc * xc, axis=-1, keepdims=True)
    return xc * lax.rsqrt(var + eps) * g + b


def _sigmoid(x):
    return 1.0 / (1.0 + jnp.exp(-x))


def _inproj_kernel(x_ref, g_ref, b_ref, w_ref, o_ref, xn_ref):
    @pl.when(pl.program_id(1) == 0)
    def _():
        g = g_ref[...]
        b = b_ref[...]

        def body(i, c):
            r0 = pl.multiple_of(i * LN_ROWS, LN_ROWS)
            xn_ref[pl.ds(r0, LN_ROWS), :] = _layer_norm(
                x_ref[pl.ds(r0, LN_ROWS), :], g, b, LN_EPS).astype(BF16)
            return c

        lax.fori_loop(0, TM_IN // LN_ROWS, body, 0)

    o_ref[...] = jnp.dot(xn_ref[...], w_ref[...], preferred_element_type=F32)


def _inproj(x2, g, b, w_bf16):
    m = x2.shape[0]
    return pl.pallas_call(
        _inproj_kernel,
        grid=(m // TM_IN, IN_PROJ_DIM // TN_IN),
        in_specs=[
            pl.BlockSpec((TM_IN, D_MODEL), lambda i, j: (i, 0)),
            pl.BlockSpec((1, D_MODEL), lambda i, j: (0, 0)),
            pl.BlockSpec((1, D_MODEL), lambda i, j: (0, 0)),
            pl.BlockSpec((D_MODEL, TN_IN), lambda i, j: (0, j)),
        ],
        out_specs=pl.BlockSpec((TM_IN, TN_IN), lambda i, j: (i, j)),
        out_shape=jax.ShapeDtypeStruct((m, IN_PROJ_DIM), F32),
        scratch_shapes=[pltpu.VMEM((TM_IN, D_MODEL), BF16)],
        compiler_params=pltpu.CompilerParams(
            dimension_semantics=("arbitrary", "arbitrary"),
            vmem_limit_bytes=VMEM_LIMIT),
        name="ln_inproj",
    )(x2, g, b, w_bf16)


def _conv_kernel(a_ref, gate_ref, w_ref, cb_ref, ng_ref, nb_ref, o_ref, ubuf_ref):
    tt = TT_CONV

    @pl.when(pl.program_id(2) == 0)
    def _():
        ubuf_ref[pl.ds(0, CONV_HALO), :] = jnp.zeros((CONV_HALO, LANES), F32)

    @pl.when(pl.program_id(2) != 0)
    def _():
        ubuf_ref[pl.ds(0, CONV_HALO), :] = ubuf_ref[pl.ds(tt, CONV_HALO), :]

    ubuf_ref[pl.ds(CONV_HALO, tt), :] = a_ref[...] * _sigmoid(gate_ref[...])

    cb = cb_ref[...]
    ng = ng_ref[...]
    nb = nb_ref[...]
    lead = CONV_HALO - (CONV_KERNEL - 1)
    for s in range(tt // RS_CONV):
        base = s * RS_CONV + lead
        acc = jnp.broadcast_to(cb, (RS_CONV, LANES))
        for r in range(SUBLANES):
            taps = list(range(r, CONV_KERNEL, SUBLANES))
            xr = ubuf_ref[pl.ds(base + r, RS_CONV + (len(taps) - 1) * SUBLANES), :]
            for a, k in enumerate(taps):
                acc = acc + w_ref[pl.ds(k, 1), :] * xr[a * SUBLANES:a * SUBLANES + RS_CONV, :]
        y = _layer_norm(acc, ng, nb, LN_EPS)
        o_ref[pl.ds(s * RS_CONV, RS_CONV), :] = (y * _sigmoid(y)).astype(BF16)


def _conv_group(proj, conv_w, conv_b, norm_g, norm_b, batch, seq):
    m = proj.shape[0]
    nt = seq // TT_CONV
    return pl.pallas_call(
        _conv_kernel,
        grid=(batch, CONV_GROUPS, nt),
        in_specs=[
            pl.BlockSpec((TT_CONV, LANES), lambda b, g, j: (b * nt + j, g)),
            pl.BlockSpec((TT_CONV, LANES), lambda b, g, j: (b * nt + j, CONV_GROUPS + g)),
            pl.BlockSpec((CONV_KERNEL, LANES), lambda b, g, j: (0, g)),
            pl.BlockSpec((1, LANES), lambda b, g, j: (0, g)),
            pl.BlockSpec((1, LANES), lambda b, g, j: (0, g)),
            pl.BlockSpec((1, LANES), lambda b, g, j: (0, g)),
        ],
        out_specs=pl.BlockSpec((TT_CONV, LANES), lambda b, g, j: (b * nt + j, g)),
        out_shape=jax.ShapeDtypeStruct((m, CONV_WIDTH), BF16),
        scratch_shapes=[pltpu.VMEM((CONV_HALO + TT_CONV, LANES), F32)],
        compiler_params=pltpu.CompilerParams(
            dimension_semantics=("arbitrary", "arbitrary", "arbitrary"),
            vmem_limit_bytes=VMEM_LIMIT),
        name="conv_group",
    )(proj, proj, conv_w, conv_b, norm_g, norm_b)


def _hgrn_constants():
    t = np.arange(CHUNK)
    sgn = np.zeros((N_LEVELS, CHUNK, LANES), np.float32)
    msk = np.zeros((N_LEVELS, CHUNK, CHUNK), np.float32)
    for l in range(N_LEVELS):
        m = 1 << l
        right = (t % (2 * m)) >= m
        sgn[l] = np.where(right, 1.0, -1.0)[:, None]
        same = (t[:, None] // (2 * m)) == (t[None, :] // (2 * m))
        msk[l] = (same & right[:, None] & (~right)[None, :]).astype(np.float32)
    tri = (t[:, None] >= t[None, :]).astype(np.float32)
    eye = np.eye(CHUNK, dtype=np.float32)
    return sgn, msk, tri, eye


def _boundary_rows(b_ref, level):
    m = 1 << level
    if m >= SUBLANES:
        parts = []
        for p in range(CHUNK // (2 * m)):
            row = b_ref[pl.ds(p * 2 * m + m - 1, 1), :]
            parts.append(jnp.broadcast_to(row, (2 * m, LANES)))
        return parts[0] if len(parts) == 1 else jnp.concatenate(parts, axis=0)
    sub = lax.broadcasted_iota(jnp.int32, (SUBLANES, LANES), 0)
    parts = []
    for v in range(CHUNK // SUBLANES):
        base = v * SUBLANES
        out = jnp.broadcast_to(b_ref[pl.ds(base + m - 1, 1), :], (SUBLANES, LANES))
        for pp in range(1, SUBLANES // (2 * m)):
            nxt = jnp.broadcast_to(b_ref[pl.ds(base + pp * 2 * m + m - 1, 1), :], (SUBLANES, LANES))
            out = jnp.where(sub >= pp * 2 * m, nxt, out)
        parts.append(out)
    return jnp.concatenate(parts, axis=0)


def _hgrn_kernel(q_ref, f_ref, i_ref, og_ref, lb_ref, gn_ref, sgn_ref, msk_ref, tri_ref, eye_ref,
                 o_ref, st_ref, b_ref):
    @pl.when(pl.program_id(2) == 0)
    def _():
        st_ref[...] = jnp.zeros((HEAD_DIM, HEAD_DIM), F32)

    lb = lb_ref[...]
    gn = gn_ref[...]
    nt = (((1,), (1,)), ((), ()))
    tn = (((0,), (0,)), ((), ()))

    def chunk(c, carry):
        r0 = pl.multiple_of(c * CHUNK, CHUNK)
        qr = q_ref[pl.ds(r0, CHUNK), :]
        fr = f_ref[pl.ds(r0, CHUNK), :]
        v = i_ref[pl.ds(r0, CHUNK), :].astype(BF16)
        fg = lb + (1.0 - lb) * _sigmoid(fr)
        g = jnp.log(fg)
        kk = 1.0 - fg
        qh = qr * _sigmoid(qr)

        g1 = g.astype(BF16)
        r1 = g - g1.astype(F32)
        g2 = r1.astype(BF16)
        g3 = (r1 - g2.astype(F32)).astype(BF16)
        tri = tri_ref[...]
        b = (jnp.dot(tri, g1, preferred_element_type=F32)
             + jnp.dot(tri, g2, preferred_element_type=F32)
             + jnp.dot(tri, g3, preferred_element_type=F32))
        b_ref[...] = b

        scores = eye_ref[...] * jnp.sum(qh * kk, axis=-1, keepdims=True)
        for l in range(N_LEVELS):
            sg = sgn_ref[l]
            if l == 0:
                z = jnp.where(sg > 0, qh * fg, kk)
            else:
                e = sg * (b - _boundary_rows(b_ref, l))
                z = jnp.where(sg > 0, qh, kk) * jnp.exp(e)
            z = z.astype(BF16)
            s_l = lax.dot_general(z, z, nt, preferred_element_type=F32)
            scores = scores + msk_ref[l] * s_l
        o_intra = jnp.dot(scores.astype(BF16), v, preferred_element_type=F32)

        st = st_ref[...]
        q_in = (qh * jnp.exp(b)).astype(BF16)
        o_inter = lax.dot_general(q_in, st.astype(BF16), nt, preferred_element_type=F32)
        b_last = b_ref[pl.ds(CHUNK - 1, 1), :]
        k_up = (kk * jnp.exp(b_last - b)).astype(BF16)
        st_ref[...] = st * jnp.exp(b_last) + lax.dot_general(v, k_up, tn, preferred_element_type=F32)

        o = o_intra + o_inter
        o = o * lax.rsqrt(jnp.mean(o * o, axis=-1, keepdims=True) + RMS_EPS) * gn
        og = og_ref[pl.ds(r0, CHUNK), :]
        o_ref[pl.ds(r0, CHUNK), :] = (o * (og * _sigmoid(og))).astype(BF16)
        return carry

    lax.fori_loop(0, TT_HGRN // CHUNK, chunk, 0)


def _hgrn_group(proj, lb, gn, batch, seq):
    m = proj.shape[0]
    nt = seq // TT_HGRN
    sgn, msk, tri, eye = _hgrn_constants()
    col0 = 2 * CONV_WIDTH // LANES

    def col(n):
        return lambda b, h, j: (b * nt + j, col0 + n * HGRN_HEADS + h)

    const3 = lambda b, h, j: (0, 0, 0)
    const2 = lambda b, h, j: (0, 0)
    return pl.pallas_call(
        _hgrn_kernel,
        grid=(batch, HGRN_HEADS, nt),
        in_specs=[
            pl.BlockSpec((TT_HGRN, LANES), col(0)),
            pl.BlockSpec((TT_HGRN, LANES), col(1)),
            pl.BlockSpec((TT_HGRN, LANES), col(2)),
            pl.BlockSpec((TT_HGRN, LANES), col(3)),
            pl.BlockSpec((1, LANES), lambda b, h, j: (0, h)),
            pl.BlockSpec((1, LANES), lambda b, h, j: (0, h)),
            pl.BlockSpec((N_LEVELS, CHUNK, LANES), const3),
            pl.BlockSpec((N_LEVELS, CHUNK, CHUNK), const3),
            pl.BlockSpec((CHUNK, CHUNK), const2),
            pl.BlockSpec((CHUNK, CHUNK), const2),
        ],
        out_specs=pl.BlockSpec((TT_HGRN, LANES), lambda b, h, j: (b * nt + j, h)),
        out_shape=jax.ShapeDtypeStruct((m, HGRN_WIDTH), BF16),
        scratch_shapes=[pltpu.VMEM((HEAD_DIM, HEAD_DIM), F32), pltpu.VMEM((CHUNK, LANES), F32)],
        compiler_params=pltpu.CompilerParams(
            dimension_semantics=("arbitrary", "arbitrary", "arbitrary"),
            vmem_limit_bytes=VMEM_LIMIT),
        name="hgrn_group",
    )(proj, proj, proj, proj, lb, gn, jnp.asarray(sgn), jnp.asarray(msk),
      jnp.asarray(tri, dtype=BF16), jnp.asarray(eye))


def _outproj_kernel(u_ref, o_ref, x_ref, g0_ref, b0_ref, w_ref, g1_ref, b1_ref, h_ref, mix_ref):
    mix_ref[...] = (
        jnp.dot(u_ref[...], w_ref[pl.ds(0, CONV_WIDTH), :], preferred_element_type=F32)
        + jnp.dot(o_ref[...], w_ref[pl.ds(CONV_WIDTH, HGRN_WIDTH), :], preferred_element_type=F32))
    g0 = g0_ref[...]
    b0 = b0_ref[...]
    g1 = g1_ref[...]
    b1 = b1_ref[...]

    def body(i, c):
        r0 = pl.multiple_of(i * LN_ROWS, LN_ROWS)
        h0 = _layer_norm(x_ref[pl.ds(r0, LN_ROWS), :], g0, b0, LN_EPS)
        y = ALPHA * h0 + mix_ref[pl.ds(r0, LN_ROWS), :]
        h_ref[pl.ds(r0, LN_ROWS), :] = _layer_norm(y, g1, b1, LN_EPS)
        return c

    lax.fori_loop(0, TM_OUT // LN_ROWS, body, 0)


def _outproj(u, o, x2, g0, b0, w_bf16, g1, b1):
    m = x2.shape[0]
    row = lambda i: (i, 0)
    const = lambda i: (0, 0)
    return pl.pallas_call(
        _outproj_kernel,
        grid=(m // TM_OUT,),
        in_specs=[
            pl.BlockSpec((TM_OUT, CONV_WIDTH), row),
            pl.BlockSpec((TM_OUT, HGRN_WIDTH), row),
            pl.BlockSpec((TM_OUT, D_MODEL), row),
            pl.BlockSpec((1, D_MODEL), const),
            pl.BlockSpec((1, D_MODEL), const),
            pl.BlockSpec((D_MODEL, D_MODEL), const),
            pl.BlockSpec((1, D_MODEL), const),
            pl.BlockSpec((1, D_MODEL), const),
        ],
        out_specs=pl.BlockSpec((TM_OUT, D_MODEL), row),
        out_shape=jax.ShapeDtypeStruct((m, D_MODEL), F32),
        scratch_shapes=[pltpu.VMEM((TM_OUT, D_MODEL), F32)],
        compiler_params=pltpu.CompilerParams(
            dimension_semantics=("arbitrary",),
            vmem_limit_bytes=VMEM_LIMIT),
        name="outproj_ln1",
    )(u, o, x2, g0, b0, w_bf16, g1, b1)


def _ffn_kernel(tiles_per_seq, h_ref, wg_ref, wv_ref, cw_ref, cb_ref, wd_ref, g2_ref, b2_ref,
                o_ref, hb_ref, gbuf_ref, tail_ref):
    i = pl.program_id(0)
    f = pl.program_id(1)
    nf = pl.num_programs(1)
    tm = TM_FFN

    @pl.when(f == 0)
    def _():
        hb_ref[...] = h_ref[...].astype(BF16)

    hb = hb_ref[...]
    gbuf_ref[pl.ds(SUBLANES, tm), :] = jnp.dot(hb, wg_ref[...], preferred_element_type=F32)
    v = jnp.dot(hb, wv_ref[...], preferred_element_type=F32)

    seq_start = (i % tiles_per_seq) == 0

    @pl.when(seq_start)
    def _():
        gbuf_ref[pl.ds(0, SUBLANES), :] = jnp.zeros((SUBLANES, TF_FFN), F32)

    @pl.when(jnp.logical_not(seq_start))
    def _():
        gbuf_ref[pl.ds(0, SUBLANES), :] = tail_ref[f]

    tail_ref[f] = gbuf_ref[pl.ds(tm, SUBLANES), :]

    conv = (cb_ref[...]
            + cw_ref[pl.ds(2, 1), :] * gbuf_ref[pl.ds(SUBLANES, tm), :]
            + cw_ref[pl.ds(1, 1), :] * gbuf_ref[pl.ds(SUBLANES - 1, tm), :]
            + cw_ref[pl.ds(0, 1), :] * gbuf_ref[pl.ds(SUBLANES - 2, tm), :])
    act = (conv * _sigmoid(conv) * v).astype(BF16)
    part = jnp.dot(act, wd_ref[...], preferred_element_type=F32)

    @pl.when(f == 0)
    def _():
        o_ref[...] = part

    @pl.when(f != 0)
    def _():
        o_ref[...] += part

    @pl.when(f == nf - 1)
    def _():
        g2 = g2_ref[...]
        b2 = b2_ref[...]

        def body(r, c):
            r0 = pl.multiple_of(r * LN_ROWS, LN_ROWS)
            y = ALPHA * h_ref[pl.ds(r0, LN_ROWS), :] + o_ref[pl.ds(r0, LN_ROWS), :]
            o_ref[pl.ds(r0, LN_ROWS), :] = _layer_norm(y, g2, b2, LN_EPS)
            return c

        lax.fori_loop(0, tm // LN_ROWS, body, 0)


def _ffn(h1, w_up_bf16, cw, cb, w_down_bf16, g2, b2, seq):
    m = h1.shape[0]
    nf = D_FF // TF_FFN
    return pl.pallas_call(
        functools.partial(_ffn_kernel, seq // TM_FFN),
        grid=(m // TM_FFN, nf),
        in_specs=[
            pl.BlockSpec((TM_FFN, D_MODEL), lambda i, f: (i, 0)),
            pl.BlockSpec((D_MODEL, TF_FFN), lambda i, f: (0, f)),
            pl.BlockSpec((D_MODEL, TF_FFN), lambda i, f: (0, nf + f)),
            pl.BlockSpec((FFN_KERNEL, TF_FFN), lambda i, f: (0, f)),
            pl.BlockSpec((1, TF_FFN), lambda i, f: (0, f)),
            pl.BlockSpec((TF_FFN, D_MODEL), lambda i, f: (f, 0)),
            pl.BlockSpec((1, D_MODEL), lambda i, f: (0, 0)),
            pl.BlockSpec((1, D_MODEL), lambda i, f: (0, 0)),
        ],
        out_specs=pl.BlockSpec((TM_FFN, D_MODEL), lambda i, f: (i, 0)),
        out_shape=jax.ShapeDtypeStruct((m, D_MODEL), F32),
        scratch_shapes=[
            pltpu.VMEM((TM_FFN, D_MODEL), BF16),
            pltpu.VMEM((SUBLANES + TM_FFN, TF_FFN), F32),
            pltpu.VMEM((nf, SUBLANES, TF_FFN), F32),
        ],
        compiler_params=pltpu.CompilerParams(
            dimension_semantics=("arbitrary", "arbitrary"),
            vmem_limit_bytes=VMEM_LIMIT),
        name="conv_ffn_ln2",
    )(h1, w_up_bf16, w_up_bf16, cw, cb, w_down_bf16, g2, b2)


def kernel(x, emb_ln_g, emb_ln_b, w_in, conv_w, conv_b, conv_norm_g, conv_norm_b, lb_logits,
           hgrn_norm_g, w_out, ln1_g, ln1_b, w_ffn_up, ffn_conv_w, ffn_conv_b, w_ffn_down,
           ln2_g, ln2_b):
    batch, seq, d = x.shape
    assert d == D_MODEL and w_in.shape[0] == DEPTH == 1
    assert seq % TT_HGRN == 0 and seq % TT_CONV == 0 and seq % TM_FFN == 0
    x2 = x.reshape(batch * seq, d)
    row = lambda a: a.reshape(1, -1).astype(F32)

    proj = _inproj(x2, row(emb_ln_g), row(emb_ln_b), w_in[0].astype(BF16))
    u = _conv_group(proj, conv_w[0], row(conv_b[0]), row(conv_norm_g[0]), row(conv_norm_b[0]),
                    batch, seq)
    lb = jnp.cumsum(jax.nn.softmax(lb_logits.astype(F32), axis=0), axis=0)[0]
    o = _hgrn_group(proj, row(lb), row(hgrn_norm_g[0]), batch, seq)
    h1 = _outproj(u, o, x2, row(emb_ln_g), row(emb_ln_b), w_out[0].astype(BF16),
                  row(ln1_g[0]), row(ln1_b[0]))
    out = _ffn(h1, w_ffn_up[0].astype(BF16), ffn_conv_w[0], row(ffn_conv_b[0]),
               w_ffn_down[0].astype(BF16), row(ln2_g[0]), row(ln2_b[0]), seq)
    return out.reshape(batch, seq, d)
```

```python
import functools

import jax
import jax.numpy as jnp
import numpy as np
from jax import lax
from jax.experimental import pallas as pl
from jax.experimental.pallas import tpu as pltpu

F32 = jnp.float32
BF16 = jnp.bfloat16

D_MODEL = 2048
CONV_WIDTH = 1024
CONV_GROUPS = 8
CONV_KERNEL = 31
HGRN_WIDTH = 1024
HGRN_HEADS = 8
HEAD_DIM = 128
IN_PROJ_DIM = 2 * CONV_WIDTH + 4 * HGRN_WIDTH
D_FF = 5632
FFN_KERNEL = 3
LN_EPS = 1e-5
RMS_EPS = 1e-6
DEPTH = 1
ALPHA = (2.0 * DEPTH) ** 0.25
LOG2E = 1.4426950408889634

LANES = 128
SUBLANES = 8
VMEM_LIMIT = 56 * 1024 * 1024

TM_IN, TN_IN = 1024, 1024
TT_CONV = 256
RS_CONV = 64
CONV_HALO = 32
TT_HGRN = 512
CHUNK = 64
N_LEVELS = 6
N_SLABS = CHUNK // SUBLANES
TM_OUT = 512
TM_FFN, TF_FFN = 512, 512
LN_ROWS = 64

_NT = (((1,), (1,)), ((), ()))
_TN = (((0,), (0,)), ((), ()))


def _layer_norm(x, g, b, eps):
    mu = jnp.mean(x, axis=-1, keepdims=True)
    xc = x - mu
    var = jnp.mean(xc * xc, axis=-1, keepdims=True)
    return xc * lax.rsqrt(var + eps) * g + b


def _sigmoid(x):
    return 1.0 / (1.0 + jnp.exp(-x))


def _inproj_kernel(x_ref, g_ref, b_ref, w_ref, o_ref, xn_ref):
    @pl.when(pl.program_id(1) == 0)
    def _():
        g = g_ref[...]
        b = b_ref[...]

        def body(i, c):
            r0 = pl.multiple_of(i * LN_ROWS, LN_ROWS)
            xn_ref[pl.ds(r0, LN_ROWS), :] = _layer_norm(
                x_ref[pl.ds(r0, LN_ROWS), :], g, b, LN_EPS).astype(BF16)
            return c

        lax.fori_loop(0, TM_IN // LN_ROWS, body, 0)

    o_ref[...] = jnp.dot(xn_ref[...], w_ref[...], preferred_element_type=F32)


def _inproj(x2, g, b, w_bf16):
    m = x2.shape[0]
    return pl.pallas_call(
        _inproj_kernel,
        grid=(m // TM_IN, IN_PROJ_DIM // TN_IN),
        in_specs=[
            pl.BlockSpec((TM_IN, D_MODEL), lambda i, j: (i, 0)),
            pl.BlockSpec((1, D_MODEL), lambda i, j: (0, 0)),
            pl.BlockSpec((1, D_MODEL), lambda i, j: (0, 0)),
            pl.BlockSpec((D_MODEL, TN_IN), lambda i, j: (0, j)),
        ],
        out_specs=pl.BlockSpec((TM_IN, TN_IN), lambda i, j: (i, j)),
        out_shape=jax.ShapeDtypeStruct((m, IN_PROJ_DIM), F32),
        scratch_shapes=[pltpu.VMEM((TM_IN, D_MODEL), BF16)],
        compiler_params=pltpu.CompilerParams(
            dimension_semantics=("arbitrary", "arbitrary"),
            vmem_limit_bytes=VMEM_LIMIT),
        name="ln_inproj",
    )(x2, g, b, w_bf16)


def _conv_kernel(a_ref, gate_ref, w_ref, cb_ref, ng_ref, nb_ref, o_ref, ubuf_ref):
    tt = TT_CONV

    @pl.when(pl.program_id(2) == 0)
    def _():
        ubuf_ref[pl.ds(0, CONV_HALO), :] = jnp.zeros((CONV_HALO, LANES), F32)

    @pl.when(pl.program_id(2) != 0)
    def _():
        ubuf_ref[pl.ds(0, CONV_HALO), :] = ubuf_ref[pl.ds(tt, CONV_HALO), :]

    ubuf_ref[pl.ds(CONV_HALO, tt), :] = a_ref[...] * _sigmoid(gate_ref[...])

    cb = cb_ref[...]
    ng = ng_ref[...]
    nb = nb_ref[...]
    lead = CONV_HALO - (CONV_KERNEL - 1)
    for s in range(tt // RS_CONV):
        base = s * RS_CONV + lead
        acc = jnp.broadcast_to(cb, (RS_CONV, LANES))
        for r in range(SUBLANES):
            taps = list(range(r, CONV_KERNEL, SUBLANES))
            xr = ubuf_ref[pl.ds(base + r, RS_CONV + (len(taps) - 1) * SUBLANES), :]
            for a, k in enumerate(taps):
                acc = acc + w_ref[pl.ds(k, 1), :] * xr[a * SUBLANES:a * SUBLANES + RS_CONV, :]
        y = _layer_norm(acc, ng, nb, LN_EPS)
        o_ref[pl.ds(s * RS_CONV, RS_CONV), :] = (y * _sigmoid(y)).astype(BF16)


def _conv_group(proj, conv_w, conv_b, norm_g, norm_b, batch, seq):
    m = proj.shape[0]
    nt = seq // TT_CONV
    return pl.pallas_call(
        _conv_kernel,
        grid=(batch, CONV_GROUPS, nt),
        in_specs=[
            pl.BlockSpec((TT_CONV, LANES), lambda b, g, j: (b * nt + j, g)),
            pl.BlockSpec((TT_CONV, LANES), lambda b, g, j: (b * nt + j, CONV_GROUPS + g)),
            pl.BlockSpec((CONV_KERNEL, LANES), lambda b, g, j: (0, g)),
            pl.BlockSpec((1, LANES), lambda b, g, j: (0, g)),
            pl.BlockSpec((1, LANES), lambda b, g, j: (0, g)),
            pl.BlockSpec((1, LANES), lambda b, g, j: (0, g)),
        ],
        out_specs=pl.BlockSpec((TT_CONV, LANES), lambda b, g, j: (b * nt + j, g)),
        out_shape=jax.ShapeDtypeStruct((m, CONV_WIDTH), BF16),
        scratch_shapes=[pltpu.VMEM((CONV_HALO + TT_CONV, LANES), F32)],
        compiler_params=pltpu.CompilerParams(
            dimension_semantics=("arbitrary", "arbitrary", "arbitrary"),
            vmem_limit_bytes=VMEM_LIMIT),
        name="conv_group",
    )(proj, proj, conv_w, conv_b, norm_g, norm_b)


def _hgrn_constants():
    t = np.arange(CHUNK)
    msk = np.zeros((N_LEVELS, CHUNK, CHUNK), np.float32)
    for l in range(N_LEVELS):
        m = 1 << l
        right = (t % (2 * m)) >= m
        same = (t[:, None] // (2 * m)) == (t[None, :] // (2 * m))
        msk[l] = (same & right[:, None] & (~right)[None, :]).astype(np.float32)
    tri = (t[:, None] >= t[None, :]).astype(np.float32)
    eye = np.eye(CHUNK, dtype=np.float32)
    return msk, tri, eye


def _hgrn_chunk_head(qr, fr, vr, ogr, lb, gn, st_ref, b_ref, msk_ref, tri, eye, sub, right_lo, sgn_lo):
    fg = lb + (1.0 - lb) * _sigmoid(fr)
    g2 = jnp.log(fg) * LOG2E
    kk = 1.0 - fg
    qh = qr * _sigmoid(qr)
    v = vr.astype(BF16)
    g_hi = g2.astype(BF16)
    g_lo = (g2 - g_hi.astype(F32)).astype(BF16)
    b = jnp.dot(tri, g_hi, preferred_element_type=F32) + jnp.dot(tri, g_lo, preferred_element_type=F32)
    b_ref[...] = b

    def slabs(x):
        return [x[i * SUBLANES:(i + 1) * SUBLANES] for i in range(N_SLABS)]

    def row(r):
        return jnp.broadcast_to(b_ref[pl.ds(r, 1), :], (SUBLANES, LANES))

    bs, qs, ks, fs = slabs(b), slabs(qh), slabs(kk), slabs(fg)
    scores = eye * jnp.sum(qh * kk, axis=-1, keepdims=True)
    for l in range(N_LEVELS):
        m = 1 << l
        zs = []
        for i in range(N_SLABS):
            base = i * SUBLANES
            if m >= SUBLANES:
                bnd = row(base - base % (2 * m) + m - 1)
                if base % (2 * m) >= m:
                    zs.append(qs[i] * jnp.exp2(bs[i] - bnd))
                else:
                    zs.append(ks[i] * jnp.exp2(bnd - bs[i]))
            elif l == 0:
                zs.append(jnp.where(right_lo[0], qs[i] * fs[i], ks[i]))
            else:
                bnd = row(base + m - 1)
                for pp in range(1, SUBLANES // (2 * m)):
                    bnd = jnp.where(sub >= pp * 2 * m, row(base + pp * 2 * m + m - 1), bnd)
                e = (bs[i] - bnd) * sgn_lo[l]
                zs.append(jnp.where(right_lo[l], qs[i], ks[i]) * jnp.exp2(e))
        z = jnp.concatenate(zs, axis=0).astype(BF16)
        s_l = lax.dot_general(z, z, _NT, preferred_element_type=F32)
        scores = scores + msk_ref[l] * s_l
    o_intra = jnp.dot(scores.astype(BF16), v, preferred_element_type=F32)

    st = st_ref[...]
    q_in = (qh * jnp.exp2(b)).astype(BF16)
    o_inter = lax.dot_general(q_in, st.astype(BF16), _NT, preferred_element_type=F32)
    b_last = b_ref[pl.ds(CHUNK - 1, 1), :]
    k_up = (kk * jnp.exp2(b_last - b)).astype(BF16)
    st_ref[...] = st * jnp.exp2(b_last) + lax.dot_general(v, k_up, _TN, preferred_element_type=F32)

    o = o_intra + o_inter
    o = o * lax.rsqrt(jnp.mean(o * o, axis=-1, keepdims=True) + RMS_EPS) * gn
    return (o * (ogr * _sigmoid(ogr))).astype(BF16)


def _hgrn_kernel(q_ref, f_ref, i_ref, og_ref, lbl_ref, gn_ref, msk_ref, tri_ref, eye_ref,
                 o_ref, st_ref, b_ref):
    @pl.when(pl.program_id(1) == 0)
    def _():
        st_ref[...] = jnp.zeros((HGRN_HEADS, HEAD_DIM, HEAD_DIM), F32)

    rows = [lbl_ref[pl.ds(r, 1), :] for r in range(DEPTH + 1)]
    mx = functools.reduce(jnp.maximum, rows)
    ex = [jnp.exp(r - mx) for r in rows]
    lb_all = ex[0] / functools.reduce(lambda a, c: a + c, ex)
    gn_all = gn_ref[...]

    sub = lax.broadcasted_iota(jnp.int32, (SUBLANES, LANES), 0)
    right_lo = [(sub % (2 << l)) >= (1 << l) for l in range(3)]
    sgn_lo = [jnp.where(r, 1.0, -1.0).astype(F32) for r in right_lo]
    tri = tri_ref[...]
    eye = eye_ref[...]

    def chunk(c, carry):
        r0 = pl.multiple_of(c * CHUNK, CHUNK)
        rws = pl.ds(r0, CHUNK)
        for h in range(HGRN_HEADS):
            cols = pl.ds(h * HEAD_DIM, HEAD_DIM)
            lane = slice(h * HEAD_DIM, (h + 1) * HEAD_DIM)
            o_ref[rws, cols] = _hgrn_chunk_head(
                q_ref[rws, cols], f_ref[rws, cols], i_ref[rws, cols], og_ref[rws, cols],
                lb_all[:, lane], gn_all[:, lane], st_ref.at[h], b_ref.at[h], msk_ref, tri, eye,
                sub, right_lo, sgn_lo)
        return carry

    lax.fori_loop(0, TT_HGRN // CHUNK, chunk, 0)


def _hgrn_group(proj, lb_logits, gn, batch, seq):
    m = proj.shape[0]
    nt = seq // TT_HGRN
    msk, tri, eye = _hgrn_constants()
    col0 = 2 * CONV_WIDTH // HGRN_WIDTH

    def col(n):
        return lambda b, j: (b * nt + j, col0 + n)

    return pl.pallas_call(
        _hgrn_kernel,
        grid=(batch, nt),
        in_specs=[
            pl.BlockSpec((TT_HGRN, HGRN_WIDTH), col(0)),
            pl.BlockSpec((TT_HGRN, HGRN_WIDTH), col(1)),
            pl.BlockSpec((TT_HGRN, HGRN_WIDTH), col(2)),
            pl.BlockSpec((TT_HGRN, HGRN_WIDTH), col(3)),
            pl.BlockSpec((DEPTH + 1, HGRN_WIDTH), lambda b, j: (0, 0)),
            pl.BlockSpec((1, HGRN_WIDTH), lambda b, j: (0, 0)),
            pl.BlockSpec((N_LEVELS, CHUNK, CHUNK), lambda b, j: (0, 0, 0)),
            pl.BlockSpec((CHUNK, CHUNK), lambda b, j: (0, 0)),
            pl.BlockSpec((CHUNK, CHUNK), lambda b, j: (0, 0)),
        ],
        out_specs=pl.BlockSpec((TT_HGRN, HGRN_WIDTH), lambda b, j: (b * nt + j, 0)),
        out_shape=jax.ShapeDtypeStruct((m, HGRN_WIDTH), BF16),
        scratch_shapes=[pltpu.VMEM((HGRN_HEADS, HEAD_DIM, HEAD_DIM), F32),
                        pltpu.VMEM((HGRN_HEADS, CHUNK, LANES), F32)],
        compiler_params=pltpu.CompilerParams(
            dimension_semantics=("arbitrary", "arbitrary"),
            vmem_limit_bytes=VMEM_LIMIT),
        name="hgrn_group",
    )(proj, proj, proj, proj, lb_logits.astype(F32), gn, jnp.asarray(msk),
      jnp.asarray(tri, dtype=BF16), jnp.asarray(eye))


def _outproj_kernel(u_ref, o_ref, x_ref, g0_ref, b0_ref, w_ref, g1_ref, b1_ref, h_ref, hb_ref, mix_ref):
    mix_ref[...] = (
        jnp.dot(u_ref[...], w_ref[pl.ds(0, CONV_WIDTH), :], preferred_element_type=F32)
        + jnp.dot(o_ref[...], w_ref[pl.ds(CONV_WIDTH, HGRN_WIDTH), :], preferred_element_type=F32))
    g0 = g0_ref[...]
    b0 = b0_ref[...]
    g1 = g1_ref[...]
    b1 = b1_ref[...]

    def body(i, c):
        r0 = pl.multiple_of(i * LN_ROWS, LN_ROWS)
        h0 = _layer_norm(x_ref[pl.ds(r0, LN_ROWS), :], g0, b0, LN_EPS)
        y = ALPHA * h0 + mix_ref[pl.ds(r0, LN_ROWS), :]
        h1 = _layer_norm(y, g1, b1, LN_EPS)
        h_ref[pl.ds(r0, LN_ROWS), :] = h1
        hb_ref[pl.ds(r0, LN_ROWS), :] = h1.astype(BF16)
        return c

    lax.fori_loop(0, TM_OUT // LN_ROWS, body, 0)


def _outproj(u, o, x2, g0, b0, w_bf16, g1, b1):
    m = x2.shape[0]
    row = lambda i: (i, 0)
    const = lambda i: (0, 0)
    return pl.pallas_call(
        _outproj_kernel,
        grid=(m // TM_OUT,),
        in_specs=[
            pl.BlockSpec((TM_OUT, CONV_WIDTH), row),
            pl.BlockSpec((TM_OUT, HGRN_WIDTH), row),
            pl.BlockSpec((TM_OUT, D_MODEL), row),
            pl.BlockSpec((1, D_MODEL), const),
            pl.BlockSpec((1, D_MODEL), const),
            pl.BlockSpec((D_MODEL, D_MODEL), const),
            pl.BlockSpec((1, D_MODEL), const),
            pl.BlockSpec((1, D_MODEL), const),
        ],
        out_specs=[pl.BlockSpec((TM_OUT, D_MODEL), row), pl.BlockSpec((TM_OUT, D_MODEL), row)],
        out_shape=[jax.ShapeDtypeStruct((m, D_MODEL), F32), jax.ShapeDtypeStruct((m, D_MODEL), BF16)],
        scratch_shapes=[pltpu.VMEM((TM_OUT, D_MODEL), F32)],
        compiler_params=pltpu.CompilerParams(
            dimension_semantics=("arbitrary",),
            vmem_limit_bytes=VMEM_LIMIT),
        name="outproj_ln1",
    )(u, o, x2, g0, b0, w_bf16, g1, b1)


def _ffn_kernel(tiles_per_seq, nf, hb_ref, hres_ref, wg_ref, wv_ref, cw_ref, cb_ref, wd_ref, g2_ref,
                b2_ref, o_ref, acc_ref, g_ref, v_ref, tail_ref):
    s = pl.program_id(0)
    tm = TM_FFN
    sb = jnp.maximum(s - 1, 0)
    i_b = sb // nf
    f_b = sb % nf

    @pl.when(s == 0)
    def _():
        acc_ref[...] = jnp.zeros(acc_ref.shape, F32)
        g_ref[...] = jnp.zeros(g_ref.shape, F32)
        v_ref[...] = jnp.zeros(v_ref.shape, F32)
        tail_ref[...] = jnp.zeros(tail_ref.shape, F32)

    def step(slot_a, slot_b):
        hb = hb_ref[...]
        g_ref[slot_a, pl.ds(SUBLANES, tm), :] = jnp.dot(hb, wg_ref[...], preferred_element_type=F32)
        v_ref[slot_a] = jnp.dot(hb, wv_ref[...], preferred_element_type=F32)

        seq_start = (i_b % tiles_per_seq) == 0
        g_ref[slot_b, pl.ds(0, SUBLANES), :] = jnp.where(seq_start, 0.0, tail_ref[f_b])
        tail_ref[f_b] = g_ref[slot_b, pl.ds(tm, SUBLANES), :]
        conv = (cb_ref[...]
                + cw_ref[pl.ds(2, 1), :] * g_ref[slot_b, pl.ds(SUBLANES, tm), :]
                + cw_ref[pl.ds(1, 1), :] * g_ref[slot_b, pl.ds(SUBLANES - 1, tm), :]
                + cw_ref[pl.ds(0, 1), :] * g_ref[slot_b, pl.ds(SUBLANES - 2, tm), :])
        act = (conv * _sigmoid(conv) * v_ref[slot_b]).astype(BF16)
        acc_ref[...] += jnp.dot(act, wd_ref[...], preferred_element_type=F32)

    @pl.when(s % 2 == 0)
    def _():
        step(0, 1)

    @pl.when(s % 2 == 1)
    def _():
        step(1, 0)

    @pl.when(jnp.logical_and(s > 0, f_b == nf - 1))
    def _():
        g2 = g2_ref[...]
        b2 = b2_ref[...]

        def body(r, c):
            r0 = pl.multiple_of(r * LN_ROWS, LN_ROWS)
            y = ALPHA * hres_ref[pl.ds(r0, LN_ROWS), :] + acc_ref[pl.ds(r0, LN_ROWS), :]
            o_ref[pl.ds(r0, LN_ROWS), :] = _layer_norm(y, g2, b2, LN_EPS)
            acc_ref[pl.ds(r0, LN_ROWS), :] = jnp.zeros((LN_ROWS, D_MODEL), F32)
            return c

        lax.fori_loop(0, tm // LN_ROWS, body, 0)


def _ffn(h1, h1_bf16, w_up_bf16, cw, cb, w_down_bf16, g2, b2, seq):
    m = h1.shape[0]
    nf = D_FF // TF_FFN
    n_steps = (m // TM_FFN) * nf
    a_step = lambda s: jnp.minimum(s, n_steps - 1)
    b_step = lambda s: jnp.maximum(s - 1, 0)
    return pl.pallas_call(
        functools.partial(_ffn_kernel, seq // TM_FFN, nf),
        grid=(n_steps + 1,),
        in_specs=[
            pl.BlockSpec((TM_FFN, D_MODEL), lambda s: (a_step(s) // nf, 0)),
            pl.BlockSpec((TM_FFN, D_MODEL), lambda s: (b_step(s) // nf, 0)),
            pl.BlockSpec((D_MODEL, TF_FFN), lambda s: (0, a_step(s) % nf)),
            pl.BlockSpec((D_MODEL, TF_FFN), lambda s: (0, nf + a_step(s) % nf)),
            pl.BlockSpec((FFN_KERNEL, TF_FFN), lambda s: (0, b_step(s) % nf)),
            pl.BlockSpec((1, TF_FFN), lambda s: (0, b_step(s) % nf)),
            pl.BlockSpec((TF_FFN, D_MODEL), lambda s: (b_step(s) % nf, 0)),
            pl.BlockSpec((1, D_MODEL), lambda s: (0, 0)),
            pl.BlockSpec((1, D_MODEL), lambda s: (0, 0)),
        ],
        out_specs=pl.BlockSpec((TM_FFN, D_MODEL), lambda s: (b_step(s) // nf, 0)),
        out_shape=jax.ShapeDtypeStruct((m, D_MODEL), F32),
        scratch_shapes=[
            pltpu.VMEM((TM_FFN, D_MODEL), F32),
            pltpu.VMEM((2, SUBLANES + TM_FFN, TF_FFN), F32),
            pltpu.VMEM((2, TM_FFN, TF_FFN), F32),
            pltpu.VMEM((nf, SUBLANES, TF_FFN), F32),
        ],
        compiler_params=pltpu.CompilerParams(
            dimension_semantics=("arbitrary",),
            vmem_limit_bytes=VMEM_LIMIT),
        name="conv_ffn_ln2",
    )(h1_bf16, h1, w_up_bf16, w_up_bf16, cw, cb, w_down_bf16, g2, b2)


def kernel(x, emb_ln_g, emb_ln_b, w_in, conv_w, conv_b, conv_norm_g, conv_norm_b, lb_logits,
           hgrn_norm_g, w_out, ln1_g, ln1_b, w_ffn_up, ffn_conv_w, ffn_conv_b, w_ffn_down,
           ln2_g, ln2_b):
    batch, seq, d = x.shape
    assert d == D_MODEL and w_in.shape[0] == DEPTH == 1
    assert seq % TT_HGRN == 0 and seq % TT_CONV == 0 and seq % TM_FFN == 0
    x2 = x.reshape(batch * seq, d)
    row = lambda a: a.reshape(1, -1).astype(F32)

    proj = _inproj(x2, row(emb_ln_g), row(emb_ln_b), w_in[0].astype(BF16))
    u = _conv_group(proj, conv_w[0], row(conv_b[0]), row(conv_norm_g[0]), row(conv_norm_b[0]),
                    batch, seq)
    o = _hgrn_group(proj, lb_logits, row(hgrn_norm_g[0]), batch, seq)
    h1, h1_bf16 = _outproj(u, o, x2, row(emb_ln_g), row(emb_ln_b), w_out[0].astype(BF16),
                           row(ln1_g[0]), row(ln1_b[0]))
    out = _ffn(h1, h1_bf16, w_ffn_up[0].astype(BF16), ffn_conv_w[0], row(ffn_conv_b[0]),
               w_ffn_down[0].astype(BF16), row(ln2_g[0]), row(ln2_b[0]), seq)
    return out.reshape(batch, seq, d)
```

```python
import functools

import jax
import jax.numpy as jnp
import numpy as np
from jax import lax
from jax.experimental import pallas as pl
from jax.experimental.pallas import tpu as pltpu

F32 = jnp.float32
BF16 = jnp.bfloat16

D_MODEL = 2048
CONV_WIDTH = 1024
CONV_GROUPS = 8
CONV_KERNEL = 31
HGRN_WIDTH = 1024
HGRN_HEADS = 8
HEAD_DIM = 128
IN_PROJ_DIM = 2 * CONV_WIDTH + 4 * HGRN_WIDTH
D_FF = 5632
FFN_KERNEL = 3
LN_EPS = 1e-5
RMS_EPS = 1e-6
DEPTH = 1
ALPHA = (2.0 * DEPTH) ** 0.25
LOG2E = 1.4426950408889634

LANES = 128
SUBLANES = 8
VMEM_LIMIT = 56 * 1024 * 1024

TM_IN, TN_IN = 1024, 1024
TT_CONV = 256
RS_CONV = 64
CONV_HALO = 32
TT_HGRN = 512
CHUNK = 64
N_LEVELS = 6
N_SLABS = CHUNK // SUBLANES
TM_OUT = 512
TM_FFN, TF_FFN = 512, 512
LN_ROWS = 64
LN_ROWS_INLINE = 16

_NT = (((1,), (1,)), ((), ()))
_TN = (((0,), (0,)), ((), ()))


def _layer_norm(x, g, b, eps):
    mu = jnp.mean(x, axis=-1, keepdims=True)
    xc = x - mu
    var = jnp.mean(xc * xc, axis=-1, keepdims=True)
    return xc * lax.rsqrt(var + eps) * g + b


def _sigmoid(x):
    return 1.0 / (1.0 + jnp.exp(-x))


def _inproj_kernel(x_ref, g_ref, b_ref, w_ref, o_ref, xn_ref):
    @pl.when(pl.program_id(1) == 0)
    def _():
        g = g_ref[...]
        b = b_ref[...]

        def body(i, c):
            r0 = pl.multiple_of(i * LN_ROWS, LN_ROWS)
            xn_ref[pl.ds(r0, LN_ROWS), :] = _layer_norm(
                x_ref[pl.ds(r0, LN_ROWS), :], g, b, LN_EPS).astype(BF16)
            return c

        lax.fori_loop(0, TM_IN // LN_ROWS, body, 0)

    o_ref[...] = jnp.dot(xn_ref[...], w_ref[...].astype(BF16),
                         preferred_element_type=F32).astype(BF16)


def _inproj(x2, g, b, w_f32):
    m = x2.shape[0]
    return pl.pallas_call(
        _inproj_kernel,
        grid=(m // TM_IN, IN_PROJ_DIM // TN_IN),
        in_specs=[
            pl.BlockSpec((TM_IN, D_MODEL), lambda i, j: (i, 0)),
            pl.BlockSpec((1, D_MODEL), lambda i, j: (0, 0)),
            pl.BlockSpec((1, D_MODEL), lambda i, j: (0, 0)),
            pl.BlockSpec((D_MODEL, TN_IN), lambda i, j: (0, j)),
        ],
        out_specs=pl.BlockSpec((TM_IN, TN_IN), lambda i, j: (i, j)),
        out_shape=jax.ShapeDtypeStruct((m, IN_PROJ_DIM), BF16),
        scratch_shapes=[pltpu.VMEM((TM_IN, D_MODEL), BF16)],
        compiler_params=pltpu.CompilerParams(
            dimension_semantics=("arbitrary", "arbitrary"),
            vmem_limit_bytes=VMEM_LIMIT),
        name="ln_inproj",
    )(x2, g, b, w_f32)


def _conv_kernel(a_ref, gate_ref, w_ref, cb_ref, ng_ref, nb_ref, o_ref, ubuf_ref):
    tt = TT_CONV

    @pl.when(pl.program_id(2) == 0)
    def _():
        ubuf_ref[pl.ds(0, CONV_HALO), :] = jnp.zeros((CONV_HALO, LANES), F32)

    @pl.when(pl.program_id(2) != 0)
    def _():
        ubuf_ref[pl.ds(0, CONV_HALO), :] = ubuf_ref[pl.ds(tt, CONV_HALO), :]

    ubuf_ref[pl.ds(CONV_HALO, tt), :] = a_ref[...].astype(F32) * _sigmoid(gate_ref[...].astype(F32))

    cb = cb_ref[...]
    ng = ng_ref[...]
    nb = nb_ref[...]
    lead = CONV_HALO - (CONV_KERNEL - 1)
    for s in range(tt // RS_CONV):
        base = s * RS_CONV + lead
        acc = jnp.broadcast_to(cb, (RS_CONV, LANES))
        for r in range(SUBLANES):
            taps = list(range(r, CONV_KERNEL, SUBLANES))
            xr = ubuf_ref[pl.ds(base + r, RS_CONV + (len(taps) - 1) * SUBLANES), :]
            for a, k in enumerate(taps):
                acc = acc + w_ref[pl.ds(k, 1), :] * xr[a * SUBLANES:a * SUBLANES + RS_CONV, :]
        y = _layer_norm(acc, ng, nb, LN_EPS)
        o_ref[pl.ds(s * RS_CONV, RS_CONV), :] = (y * _sigmoid(y)).astype(BF16)


def _conv_group(proj, conv_w, conv_b, norm_g, norm_b, batch, seq):
    m = proj.shape[0]
    nt = seq // TT_CONV
    return pl.pallas_call(
        _conv_kernel,
        grid=(batch, CONV_GROUPS, nt),
        in_specs=[
            pl.BlockSpec((TT_CONV, LANES), lambda b, g, j: (b * nt + j, g)),
            pl.BlockSpec((TT_CONV, LANES), lambda b, g, j: (b * nt + j, CONV_GROUPS + g)),
            pl.BlockSpec((CONV_KERNEL, LANES), lambda b, g, j: (0, g)),
            pl.BlockSpec((1, LANES), lambda b, g, j: (0, g)),
            pl.BlockSpec((1, LANES), lambda b, g, j: (0, g)),
            pl.BlockSpec((1, LANES), lambda b, g, j: (0, g)),
        ],
        out_specs=pl.BlockSpec((TT_CONV, LANES), lambda b, g, j: (b * nt + j, g)),
        out_shape=jax.ShapeDtypeStruct((m, CONV_WIDTH), BF16),
        scratch_shapes=[pltpu.VMEM((CONV_HALO + TT_CONV, LANES), F32)],
        compiler_params=pltpu.CompilerParams(
            dimension_semantics=("arbitrary", "arbitrary", "arbitrary"),
            vmem_limit_bytes=VMEM_LIMIT),
        name="conv_group",
    )(proj, proj, conv_w, conv_b, norm_g, norm_b)


def _hgrn_constants():
    t = np.arange(CHUNK)
    msk = np.zeros((N_LEVELS, CHUNK, CHUNK), np.float32)
    for l in range(N_LEVELS):
        m = 1 << l
        right = (t % (2 * m)) >= m
        same = (t[:, None] // (2 * m)) == (t[None, :] // (2 * m))
        msk[l] = (same & right[:, None] & (~right)[None, :]).astype(np.float32)
    tri = (t[:, None] >= t[None, :]).astype(np.float32)
    eye = np.eye(CHUNK, dtype=np.float32)
    return msk, tri, eye


def _hgrn_chunk_head(qr, fr, vr, ogr, lb, gn, st_ref, b_ref, msk_ref, tri, eye, sub, right_lo, sgn_lo):
    fg = lb + (1.0 - lb) * _sigmoid(fr)
    g2 = jnp.log(fg) * LOG2E
    kk = 1.0 - fg
    qh = qr * _sigmoid(qr)
    v = vr.astype(BF16)
    g_hi = g2.astype(BF16)
    g_lo = (g2 - g_hi.astype(F32)).astype(BF16)
    b = jnp.dot(tri, g_hi, preferred_element_type=F32) + jnp.dot(tri, g_lo, preferred_element_type=F32)
    b_ref[...] = b

    def slabs(x):
        return [x[i * SUBLANES:(i + 1) * SUBLANES] for i in range(N_SLABS)]

    def row(r):
        return jnp.broadcast_to(b_ref[pl.ds(r, 1), :], (SUBLANES, LANES))

    bs, qs, ks, fs = slabs(b), slabs(qh), slabs(kk), slabs(fg)
    scores = eye * jnp.sum(qh * kk, axis=-1, keepdims=True)
    for l in range(N_LEVELS):
        m = 1 << l
        zs = []
        for i in range(N_SLABS):
            base = i * SUBLANES
            if m >= SUBLANES:
                bnd = row(base - base % (2 * m) + m - 1)
                if base % (2 * m) >= m:
                    zs.append(qs[i] * jnp.exp2(bs[i] - bnd))
                else:
                    zs.append(ks[i] * jnp.exp2(bnd - bs[i]))
            elif l == 0:
                zs.append(jnp.where(right_lo[0], qs[i] * fs[i], ks[i]))
            else:
                bnd = row(base + m - 1)
                for pp in range(1, SUBLANES // (2 * m)):
                    bnd = jnp.where(sub >= pp * 2 * m, row(base + pp * 2 * m + m - 1), bnd)
                e = (bs[i] - bnd) * sgn_lo[l]
                zs.append(jnp.where(right_lo[l], qs[i], ks[i]) * jnp.exp2(e))
        z = jnp.concatenate(zs, axis=0).astype(BF16)
        s_l = lax.dot_general(z, z, _NT, preferred_element_type=F32)
        scores = scores + msk_ref[l] * s_l
    o_intra = jnp.dot(scores.astype(BF16), v, preferred_element_type=F32)

    st = st_ref[...]
    q_in = (qh * jnp.exp2(b)).astype(BF16)
    o_inter = lax.dot_general(q_in, st.astype(BF16), _NT, preferred_element_type=F32)
    b_last = b_ref[pl.ds(CHUNK - 1, 1), :]
    k_up = (kk * jnp.exp2(b_last - b)).astype(BF16)
    st_ref[...] = st * jnp.exp2(b_last) + lax.dot_general(v, k_up, _TN, preferred_element_type=F32)

    o = o_intra + o_inter
    o = o * lax.rsqrt(jnp.mean(o * o, axis=-1, keepdims=True) + RMS_EPS) * gn
    return (o * (ogr * _sigmoid(ogr))).astype(BF16)


def _hgrn_kernel(q_ref, f_ref, i_ref, og_ref, lbl_ref, gn_ref, msk_ref, tri_ref, eye_ref, wu_ref, wd_ref,
                 o_ref, wub_ref, wdb_ref, st_ref, b_ref):
    @pl.when(pl.program_id(1) == 0)
    def _():
        st_ref[...] = jnp.zeros((HGRN_HEADS, HEAD_DIM, HEAD_DIM), F32)

    wub_ref[...] = wu_ref[...].astype(BF16)
    wdb_ref[...] = wd_ref[...].astype(BF16)

    rows = [lbl_ref[pl.ds(r, 1), :] for r in range(DEPTH + 1)]
    mx = functools.reduce(jnp.maximum, rows)
    ex = [jnp.exp(r - mx) for r in rows]
    lb_all = ex[0] / functools.reduce(lambda a, c: a + c, ex)
    gn_all = gn_ref[...]

    sub = lax.broadcasted_iota(jnp.int32, (SUBLANES, LANES), 0)
    right_lo = [(sub % (2 << l)) >= (1 << l) for l in range(3)]
    sgn_lo = [jnp.where(r, 1.0, -1.0).astype(F32) for r in right_lo]
    tri = tri_ref[...]
    eye = eye_ref[...]

    def chunk(c, carry):
        r0 = pl.multiple_of(c * CHUNK, CHUNK)
        rws = pl.ds(r0, CHUNK)
        for h in range(HGRN_HEADS):
            cols = pl.ds(h * HEAD_DIM, HEAD_DIM)
            lane = slice(h * HEAD_DIM, (h + 1) * HEAD_DIM)
            o_ref[rws, cols] = _hgrn_chunk_head(
                q_ref[rws, cols].astype(F32), f_ref[rws, cols].astype(F32), i_ref[rws, cols],
                og_ref[rws, cols].astype(F32), lb_all[:, lane], gn_all[:, lane], st_ref.at[h], b_ref.at[h], msk_ref, tri, eye,
                sub, right_lo, sgn_lo)
        return carry

    lax.fori_loop(0, TT_HGRN // CHUNK, chunk, 0)


def _hgrn_group(proj, lb_logits, gn, w_up, w_down, batch, seq):
    m = proj.shape[0]
    nt = seq // TT_HGRN
    n_steps = batch * nt
    msk, tri, eye = _hgrn_constants()
    col0 = 2 * CONV_WIDTH // HGRN_WIDTH
    up_rows = w_up.shape[0] // n_steps
    down_rows = w_down.shape[0] // n_steps
    assert up_rows * n_steps == w_up.shape[0] and down_rows * n_steps == w_down.shape[0]
    assert up_rows % 16 == 0 and down_rows % 16 == 0

    def col(n):
        return lambda b, j: (b * nt + j, col0 + n)

    slab = lambda b, j: (b * nt + j, 0)
    return pl.pallas_call(
        _hgrn_kernel,
        grid=(batch, nt),
        in_specs=[
            pl.BlockSpec((TT_HGRN, HGRN_WIDTH), col(0)),
            pl.BlockSpec((TT_HGRN, HGRN_WIDTH), col(1)),
            pl.BlockSpec((TT_HGRN, HGRN_WIDTH), col(2)),
            pl.BlockSpec((TT_HGRN, HGRN_WIDTH), col(3)),
            pl.BlockSpec((DEPTH + 1, HGRN_WIDTH), lambda b, j: (0, 0)),
            pl.BlockSpec((1, HGRN_WIDTH), lambda b, j: (0, 0)),
            pl.BlockSpec((N_LEVELS, CHUNK, CHUNK), lambda b, j: (0, 0, 0)),
            pl.BlockSpec((CHUNK, CHUNK), lambda b, j: (0, 0)),
            pl.BlockSpec((CHUNK, CHUNK), lambda b, j: (0, 0)),
            pl.BlockSpec((up_rows, w_up.shape[1]), slab),
            pl.BlockSpec((down_rows, w_down.shape[1]), slab),
        ],
        out_specs=[
            pl.BlockSpec((TT_HGRN, HGRN_WIDTH), lambda b, j: (b * nt + j, 0)),
            pl.BlockSpec((up_rows, w_up.shape[1]), slab),
            pl.BlockSpec((down_rows, w_down.shape[1]), slab),
        ],
        out_shape=[
            jax.ShapeDtypeStruct((m, HGRN_WIDTH), BF16),
            jax.ShapeDtypeStruct(w_up.shape, BF16),
            jax.ShapeDtypeStruct(w_down.shape, BF16),
        ],
        scratch_shapes=[pltpu.VMEM((HGRN_HEADS, HEAD_DIM, HEAD_DIM), F32),
                        pltpu.VMEM((HGRN_HEADS, CHUNK, LANES), F32)],
        compiler_params=pltpu.CompilerParams(
            dimension_semantics=("arbitrary", "arbitrary"),
            vmem_limit_bytes=VMEM_LIMIT),
        name="hgrn_group",
    )(proj, proj, proj, proj, lb_logits.astype(F32), gn, jnp.asarray(msk),
      jnp.asarray(tri, dtype=BF16), jnp.asarray(eye), w_up, w_down)


def _outproj_kernel(u_ref, o_ref, x_ref, g0_ref, b0_ref, w_ref, g1_ref, b1_ref, h_ref, hb_ref, mix_ref):
    s = pl.program_id(0)

    @pl.when(s == 0)
    def _():
        mix_ref[...] = jnp.zeros(mix_ref.shape, F32)

    def step(slot_a, slot_b):
        mix_ref[slot_a] = (
            jnp.dot(u_ref[...], w_ref[pl.ds(0, CONV_WIDTH), :], preferred_element_type=F32)
            + jnp.dot(o_ref[...], w_ref[pl.ds(CONV_WIDTH, HGRN_WIDTH), :], preferred_element_type=F32))
        g0 = g0_ref[...]
        b0 = b0_ref[...]
        g1 = g1_ref[...]
        b1 = b1_ref[...]
        for r in range(TM_OUT // LN_ROWS_INLINE):
            rows = pl.ds(r * LN_ROWS_INLINE, LN_ROWS_INLINE)
            h0 = _layer_norm(x_ref[rows, :], g0, b0, LN_EPS)
            y = ALPHA * h0 + mix_ref[slot_b, rows, :]
            h1 = _layer_norm(y, g1, b1, LN_EPS)
            h_ref[rows, :] = h1
            hb_ref[rows, :] = h1.astype(BF16)

    @pl.when(s % 2 == 0)
    def _():
        step(0, 1)

    @pl.when(s % 2 == 1)
    def _():
        step(1, 0)


def _outproj(u, o, x2, g0, b0, w_bf16, g1, b1):
    m = x2.shape[0]
    n_tiles = m // TM_OUT
    row_a = lambda s: (jnp.minimum(s, n_tiles - 1), 0)
    row = lambda s: (jnp.maximum(s - 1, 0), 0)
    const = lambda s: (0, 0)
    return pl.pallas_call(
        _outproj_kernel,
        grid=(n_tiles + 1,),
        in_specs=[
            pl.BlockSpec((TM_OUT, CONV_WIDTH), row_a),
            pl.BlockSpec((TM_OUT, HGRN_WIDTH), row_a),
            pl.BlockSpec((TM_OUT, D_MODEL), row),
            pl.BlockSpec((1, D_MODEL), const),
            pl.BlockSpec((1, D_MODEL), const),
            pl.BlockSpec((D_MODEL, D_MODEL), const),
            pl.BlockSpec((1, D_MODEL), const),
            pl.BlockSpec((1, D_MODEL), const),
        ],
        out_specs=[pl.BlockSpec((TM_OUT, D_MODEL), row), pl.BlockSpec((TM_OUT, D_MODEL), row)],
        out_shape=[jax.ShapeDtypeStruct((m, D_MODEL), F32), jax.ShapeDtypeStruct((m, D_MODEL), BF16)],
        scratch_shapes=[pltpu.VMEM((2, TM_OUT, D_MODEL), F32)],
        compiler_params=pltpu.CompilerParams(
            dimension_semantics=("arbitrary",),
            vmem_limit_bytes=VMEM_LIMIT),
        name="outproj_ln1",
    )(u, o, x2, g0, b0, w_bf16, g1, b1)


def _ffn_kernel(tiles_per_seq, nf, hb_ref, hres_ref, wg_ref, wv_ref, cw_ref, cb_ref, wd_ref, g2_ref,
                b2_ref, o_ref, acc_ref, g_ref, v_ref, tail_ref):
    s = pl.program_id(0)
    tm = TM_FFN
    sb = jnp.maximum(s - 1, 0)
    i_b = sb // nf
    f_b = sb % nf

    @pl.when(s == 0)
    def _():
        acc_ref[...] = jnp.zeros(acc_ref.shape, F32)
        g_ref[...] = jnp.zeros(g_ref.shape, F32)
        v_ref[...] = jnp.zeros(v_ref.shape, F32)
        tail_ref[...] = jnp.zeros(tail_ref.shape, F32)

    def step(slot_a, slot_b):
        hb = hb_ref[...]
        g_ref[slot_a, pl.ds(SUBLANES, tm), :] = jnp.dot(hb, wg_ref[...], preferred_element_type=F32)
        v_ref[slot_a] = jnp.dot(hb, wv_ref[...], preferred_element_type=F32)

        seq_start = (i_b % tiles_per_seq) == 0
        g_ref[slot_b, pl.ds(0, SUBLANES), :] = jnp.where(seq_start, 0.0, tail_ref[f_b])
        tail_ref[f_b] = g_ref[slot_b, pl.ds(tm, SUBLANES), :]
        conv = (cb_ref[...]
                + cw_ref[pl.ds(2, 1), :] * g_ref[slot_b, pl.ds(SUBLANES, tm), :]
                + cw_ref[pl.ds(1, 1), :] * g_ref[slot_b, pl.ds(SUBLANES - 1, tm), :]
                + cw_ref[pl.ds(0, 1), :] * g_ref[slot_b, pl.ds(SUBLANES - 2, tm), :])
        act = (conv * _sigmoid(conv) * v_ref[slot_b]).astype(BF16)
        acc_ref[...] += jnp.dot(act, wd_ref[...], preferred_element_type=F32)

    @pl.when(s % 2 == 0)
    def _():
        step(0, 1)

    @pl.when(s % 2 == 1)
    def _():
        step(1, 0)

    @pl.when(jnp.logical_and(s > 0, f_b == nf - 1))
    def _():
        g2 = g2_ref[...]
        b2 = b2_ref[...]

        def body(r, c):
            r0 = pl.multiple_of(r * LN_ROWS, LN_ROWS)
            y = ALPHA * hres_ref[pl.ds(r0, LN_ROWS), :] + acc_ref[pl.ds(r0, LN_ROWS), :]
            o_ref[pl.ds(r0, LN_ROWS), :] = _layer_norm(y, g2, b2, LN_EPS)
            acc_ref[pl.ds(r0, LN_ROWS), :] = jnp.zeros((LN_ROWS, D_MODEL), F32)
            return c

        lax.fori_loop(0, tm // LN_ROWS, body, 0)


def _ffn(h1, h1_bf16, w_up_bf16, cw, cb, w_down_bf16, g2, b2, seq):
    m = h1.shape[0]
    nf = D_FF // TF_FFN
    n_steps = (m // TM_FFN) * nf
    a_step = lambda s: jnp.minimum(s, n_steps - 1)
    b_step = lambda s: jnp.maximum(s - 1, 0)
    return pl.pallas_call(
        functools.partial(_ffn_kernel, seq // TM_FFN, nf),
        grid=(n_steps + 1,),
        in_specs=[
            pl.BlockSpec((TM_FFN, D_MODEL), lambda s: (a_step(s) // nf, 0)),
            pl.BlockSpec((TM_FFN, D_MODEL), lambda s: (b_step(s) // nf, 0)),
            pl.BlockSpec((D_MODEL, TF_FFN), lambda s: (0, a_step(s) % nf)),
            pl.BlockSpec((D_MODEL, TF_FFN), lambda s: (0, nf + a_step(s) % nf)),
            pl.BlockSpec((FFN_KERNEL, TF_FFN), lambda s: (0, b_step(s) % nf)),
            pl.BlockSpec((1, TF_FFN), lambda s: (0, b_step(s) % nf)),
            pl.BlockSpec((TF_FFN, D_MODEL), lambda s: (b_step(s) % nf, 0)),
            pl.BlockSpec((1, D_MODEL), lambda s: (0, 0)),
            pl.BlockSpec((1, D_MODEL), lambda s: (0, 0)),
        ],
        out_specs=pl.BlockSpec((TM_FFN, D_MODEL), lambda s: (b_step(s) // nf, 0)),
        out_shape=jax.ShapeDtypeStruct((m, D_MODEL), F32),
        scratch_shapes=[
            pltpu.VMEM((TM_FFN, D_MODEL), F32),
            pltpu.VMEM((2, SUBLANES + TM_FFN, TF_FFN), F32),
            pltpu.VMEM((2, TM_FFN, TF_FFN), F32),
            pltpu.VMEM((nf, SUBLANES, TF_FFN), F32),
        ],
        compiler_params=pltpu.CompilerParams(
            dimension_semantics=("arbitrary",),
            vmem_limit_bytes=VMEM_LIMIT),
        name="conv_ffn_ln2",
    )(h1_bf16, h1, w_up_bf16, w_up_bf16, cw, cb, w_down_bf16, g2, b2)


def kernel(x, emb_ln_g, emb_ln_b, w_in, conv_w, conv_b, conv_norm_g, conv_norm_b, lb_logits,
           hgrn_norm_g, w_out, ln1_g, ln1_b, w_ffn_up, ffn_conv_w, ffn_conv_b, w_ffn_down,
           ln2_g, ln2_b):
    batch, seq, d = x.shape
    assert d == D_MODEL and w_in.shape[0] == DEPTH == 1
    assert seq % TT_HGRN == 0 and seq % TT_CONV == 0 and seq % TM_FFN == 0
    x2 = x.reshape(batch * seq, d)
    row = lambda a: a.reshape(1, -1).astype(F32)

    proj = _inproj(x2, row(emb_ln_g), row(emb_ln_b), w_in[0])
    u = _conv_group(proj, conv_w[0], row(conv_b[0]), row(conv_norm_g[0]), row(conv_norm_b[0]),
                    batch, seq)
    o, w_up_bf16, w_down_bf16 = _hgrn_group(proj, lb_logits, row(hgrn_norm_g[0]), w_ffn_up[0],
                                            w_ffn_down[0], batch, seq)
    h1, h1_bf16 = _outproj(u, o, x2, row(emb_ln_g), row(emb_ln_b), w_out[0].astype(BF16),
                           row(ln1_g[0]), row(ln1_b[0]))
    out = _ffn(h1, h1_bf16, w_up_bf16, ffn_conv_w[0], row(ffn_conv_b[0]),
               w_down_bf16, row(ln2_g[0]), row(ln2_b[0]), seq)
    return out.reshape(batch, seq, d)
```

```python
import functools

import jax
import jax.numpy as jnp
import numpy as np
from jax import lax
from jax.experimental import pallas as pl
from jax.experimental.pallas import tpu as pltpu

F32 = jnp.float32
BF16 = jnp.bfloat16

D_MODEL = 2048
CONV_WIDTH = 1024
CONV_GROUPS = 8
CONV_KERNEL = 31
HGRN_WIDTH = 1024
HGRN_HEADS = 8
HEAD_DIM = 128
IN_PROJ_DIM = 2 * CONV_WIDTH + 4 * HGRN_WIDTH
D_FF = 5632
FFN_KERNEL = 3
LN_EPS = 1e-5
RMS_EPS = 1e-6
DEPTH = 1
ALPHA = (2.0 * DEPTH) ** 0.25
LOG2E = 1.4426950408889634

LANES = 128
SUBLANES = 8
VMEM_LIMIT = 56 * 1024 * 1024

TM_IN, TN_IN = 1024, 1024
CONV_HALO = 32
TT_HGRN = 512
CHUNK = 64
N_LEVELS = 6
N_SLABS = CHUNK // SUBLANES
TM_OUT = 512
TM_FFN, TF_FFN = 512, 512
LN_ROWS = 64
LN_ROWS_INLINE = 16

_NT = (((1,), (1,)), ((), ()))
_TN = (((0,), (0,)), ((), ()))


def _layer_norm(x, g, b, eps):
    mu = jnp.mean(x, axis=-1, keepdims=True)
    xc = x - mu
    var = jnp.mean(xc * xc, axis=-1, keepdims=True)
    return xc * lax.rsqrt(var + eps) * g + b


def _sigmoid(x):
    return 1.0 / (1.0 + jnp.exp(-x))


def _inproj_kernel(x_ref, g_ref, b_ref, w_ref, o_ref, xn_ref):
    @pl.when(pl.program_id(1) == 0)
    def _():
        g = g_ref[...]
        b = b_ref[...]

        def body(i, c):
            r0 = pl.multiple_of(i * LN_ROWS, LN_ROWS)
            xn_ref[pl.ds(r0, LN_ROWS), :] = _layer_norm(
                x_ref[pl.ds(r0, LN_ROWS), :], g, b, LN_EPS).astype(BF16)
            return c

        lax.fori_loop(0, TM_IN // LN_ROWS, body, 0)

    o_ref[...] = jnp.dot(xn_ref[...], w_ref[...].astype(BF16),
                         preferred_element_type=F32).astype(BF16)


def _inproj(x2, g, b, w_f32):
    m = x2.shape[0]
    return pl.pallas_call(
        _inproj_kernel,
        grid=(m // TM_IN, IN_PROJ_DIM // TN_IN),
        in_specs=[
            pl.BlockSpec((TM_IN, D_MODEL), lambda i, j: (i, 0)),
            pl.BlockSpec((1, D_MODEL), lambda i, j: (0, 0)),
            pl.BlockSpec((1, D_MODEL), lambda i, j: (0, 0)),
            pl.BlockSpec((D_MODEL, TN_IN), lambda i, j: (0, j)),
        ],
        out_specs=pl.BlockSpec((TM_IN, TN_IN), lambda i, j: (i, j)),
        out_shape=jax.ShapeDtypeStruct((m, IN_PROJ_DIM), BF16),
        scratch_shapes=[pltpu.VMEM((TM_IN, D_MODEL), BF16)],
        compiler_params=pltpu.CompilerParams(
            dimension_semantics=("arbitrary", "arbitrary"),
            vmem_limit_bytes=VMEM_LIMIT),
        name="ln_inproj",
    )(x2, g, b, w_f32)


def _hgrn_constants():
    t = np.arange(CHUNK)
    msk = np.zeros((N_LEVELS, CHUNK, CHUNK), np.float32)
    for l in range(N_LEVELS):
        m = 1 << l
        right = (t % (2 * m)) >= m
        same = (t[:, None] // (2 * m)) == (t[None, :] // (2 * m))
        msk[l] = (same & right[:, None] & (~right)[None, :]).astype(np.float32)
    tri = (t[:, None] >= t[None, :]).astype(np.float32)
    eye = np.eye(CHUNK, dtype=np.float32)
    return msk, tri, eye


def _hgrn_chunk_head(qr, fr, vr, ogr, lb, gn, st_ref, b_ref, msk_ref, tri, eye, sub, right_lo, sgn_lo):
    fg = lb + (1.0 - lb) * _sigmoid(fr)
    g2 = jnp.log(fg) * LOG2E
    kk = 1.0 - fg
    qh = qr * _sigmoid(qr)
    v = vr.astype(BF16)
    g_hi = g2.astype(BF16)
    g_lo = (g2 - g_hi.astype(F32)).astype(BF16)
    b = jnp.dot(tri, g_hi, preferred_element_type=F32) + jnp.dot(tri, g_lo, preferred_element_type=F32)
    b_ref[...] = b

    def slabs(x):
        return [x[i * SUBLANES:(i + 1) * SUBLANES] for i in range(N_SLABS)]

    def row(r):
        return jnp.broadcast_to(b_ref[pl.ds(r, 1), :], (SUBLANES, LANES))

    bs, qs, ks, fs = slabs(b), slabs(qh), slabs(kk), slabs(fg)
    scores = eye * jnp.sum(qh * kk, axis=-1, keepdims=True)
    for l in range(N_LEVELS):
        m = 1 << l
        zs = []
        for i in range(N_SLABS):
            base = i * SUBLANES
            if m >= SUBLANES:
                bnd = row(base - base % (2 * m) + m - 1)
                if base % (2 * m) >= m:
                    zs.append(qs[i] * jnp.exp2(bs[i] - bnd))
                else:
                    zs.append(ks[i] * jnp.exp2(bnd - bs[i]))
            elif l == 0:
                zs.append(jnp.where(right_lo[0], qs[i] * fs[i], ks[i]))
            else:
                bnd = row(base + m - 1)
                for pp in range(1, SUBLANES // (2 * m)):
                    bnd = jnp.where(sub >= pp * 2 * m, row(base + pp * 2 * m + m - 1), bnd)
                e = (bs[i] - bnd) * sgn_lo[l]
                zs.append(jnp.where(right_lo[l], qs[i], ks[i]) * jnp.exp2(e))
        z = jnp.concatenate(zs, axis=0).astype(BF16)
        s_l = lax.dot_general(z, z, _NT, preferred_element_type=F32)
        scores = scores + msk_ref[l] * s_l
    o_intra = jnp.dot(scores.astype(BF16), v, preferred_element_type=F32)

    st = st_ref[...]
    q_in = (qh * jnp.exp2(b)).astype(BF16)
    o_inter = lax.dot_general(q_in, st.astype(BF16), _NT, preferred_element_type=F32)
    b_last = b_ref[pl.ds(CHUNK - 1, 1), :]
    k_up = (kk * jnp.exp2(b_last - b)).astype(BF16)
    st_ref[...] = st * jnp.exp2(b_last) + lax.dot_general(v, k_up, _TN, preferred_element_type=F32)

    o = o_intra + o_inter
    o = o * lax.rsqrt(jnp.mean(o * o, axis=-1, keepdims=True) + RMS_EPS) * gn
    return (o * (ogr * _sigmoid(ogr))).astype(BF16)


def _conv_chunk_group(ubuf_ref, r0, cols, w_ref, cb, ng, nb):
    base = r0 + (CONV_HALO - (CONV_KERNEL - 1))
    acc = jnp.broadcast_to(cb, (CHUNK, LANES))
    for r in range(SUBLANES):
        taps = list(range(r, CONV_KERNEL, SUBLANES))
        xr = ubuf_ref[pl.ds(base + r, CHUNK + (len(taps) - 1) * SUBLANES), :]
        for a, k in enumerate(taps):
            acc = acc + w_ref[pl.ds(k, 1), cols] * xr[a * SUBLANES:a * SUBLANES + CHUNK, :]
    y = _layer_norm(acc, ng, nb, LN_EPS)
    return (y * _sigmoid(y)).astype(BF16)


def _mixer_kernel(a_ref, gate_ref, q_ref, f_ref, i_ref, og_ref, cw_ref, ccb_ref, cng_ref, cnb_ref,
                  lbl_ref, gn_ref, msk_ref, tri_ref, eye_ref, wu_ref, wd_ref,
                  u_ref, o_ref, wub_ref, wdb_ref, st_ref, b_ref, ubuf_ref):
    tt = TT_HGRN

    @pl.when(pl.program_id(1) == 0)
    def _():
        st_ref[...] = jnp.zeros((HGRN_HEADS, HEAD_DIM, HEAD_DIM), F32)
        ubuf_ref[:, pl.ds(0, CONV_HALO), :] = jnp.zeros((CONV_GROUPS, CONV_HALO, LANES), F32)

    @pl.when(pl.program_id(1) != 0)
    def _():
        ubuf_ref[:, pl.ds(0, CONV_HALO), :] = ubuf_ref[:, pl.ds(tt, CONV_HALO), :]

    wub_ref[...] = wu_ref[...].astype(BF16)
    wdb_ref[...] = wd_ref[...].astype(BF16)

    rows = [lbl_ref[pl.ds(r, 1), :] for r in range(DEPTH + 1)]
    mx = functools.reduce(jnp.maximum, rows)
    ex = [jnp.exp(r - mx) for r in rows]
    lb_all = ex[0] / functools.reduce(lambda a, c: a + c, ex)
    gn_all = gn_ref[...]

    sub = lax.broadcasted_iota(jnp.int32, (SUBLANES, LANES), 0)
    right_lo = [(sub % (2 << l)) >= (1 << l) for l in range(3)]
    sgn_lo = [jnp.where(r, 1.0, -1.0).astype(F32) for r in right_lo]
    tri = tri_ref[...]
    eye = eye_ref[...]
    ccb_all = ccb_ref[...]
    cng_all = cng_ref[...]
    cnb_all = cnb_ref[...]

    for c in range(tt // CHUNK):
        r0 = c * CHUNK
        rws = pl.ds(r0, CHUNK)
        for h in range(HGRN_HEADS):
            cols = pl.ds(h * HEAD_DIM, HEAD_DIM)
            lane = slice(h * HEAD_DIM, (h + 1) * HEAD_DIM)
            ubuf_ref[h, pl.ds(CONV_HALO + r0, CHUNK), :] = (
                a_ref[rws, cols].astype(F32) * _sigmoid(gate_ref[rws, cols].astype(F32)))
            o_ref[rws, cols] = _hgrn_chunk_head(
                q_ref[rws, cols].astype(F32), f_ref[rws, cols].astype(F32), i_ref[rws, cols],
                og_ref[rws, cols].astype(F32), lb_all[:, lane], gn_all[:, lane], st_ref.at[h],
                b_ref.at[h], msk_ref, tri, eye, sub, right_lo, sgn_lo)
            u_ref[rws, cols] = _conv_chunk_group(ubuf_ref.at[h], r0, cols, cw_ref, ccb_all[:, lane],
                                                 cng_all[:, lane], cnb_all[:, lane])


def _mixer(proj, conv_w, conv_b, conv_ng, conv_nb, lb_logits, gn, w_up, w_down, batch, seq):
    assert CONV_GROUPS == HGRN_HEADS and CONV_WIDTH == HGRN_WIDTH
    m = proj.shape[0]
    nt = seq // TT_HGRN
    n_steps = batch * nt
    msk, tri, eye = _hgrn_constants()
    col0 = 2 * CONV_WIDTH // HGRN_WIDTH
    up_rows = w_up.shape[0] // n_steps
    down_rows = w_down.shape[0] // n_steps
    assert up_rows * n_steps == w_up.shape[0] and down_rows * n_steps == w_down.shape[0]
    assert up_rows % 16 == 0 and down_rows % 16 == 0

    def col(n):
        return lambda b, j: (b * nt + j, n)

    slab = lambda b, j: (b * nt + j, 0)
    const2 = lambda b, j: (0, 0)
    return pl.pallas_call(
        _mixer_kernel,
        grid=(batch, nt),
        in_specs=[
            pl.BlockSpec((TT_HGRN, CONV_WIDTH), col(0)),
            pl.BlockSpec((TT_HGRN, CONV_WIDTH), col(1)),
            pl.BlockSpec((TT_HGRN, HGRN_WIDTH), col(col0)),
            pl.BlockSpec((TT_HGRN, HGRN_WIDTH), col(col0 + 1)),
            pl.BlockSpec((TT_HGRN, HGRN_WIDTH), col(col0 + 2)),
            pl.BlockSpec((TT_HGRN, HGRN_WIDTH), col(col0 + 3)),
            pl.BlockSpec((CONV_KERNEL, CONV_WIDTH), const2),
            pl.BlockSpec((1, CONV_WIDTH), const2),
            pl.BlockSpec((1, CONV_WIDTH), const2),
            pl.BlockSpec((1, CONV_WIDTH), const2),
            pl.BlockSpec((DEPTH + 1, HGRN_WIDTH), lambda b, j: (0, 0)),
            pl.BlockSpec((1, HGRN_WIDTH), lambda b, j: (0, 0)),
            pl.BlockSpec((N_LEVELS, CHUNK, CHUNK), lambda b, j: (0, 0, 0)),
            pl.BlockSpec((CHUNK, CHUNK), lambda b, j: (0, 0)),
            pl.BlockSpec((CHUNK, CHUNK), lambda b, j: (0, 0)),
            pl.BlockSpec((up_rows, w_up.shape[1]), slab),
            pl.BlockSpec((down_rows, w_down.shape[1]), slab),
        ],
        out_specs=[
            pl.BlockSpec((TT_HGRN, CONV_WIDTH), slab),
            pl.BlockSpec((TT_HGRN, HGRN_WIDTH), slab),
            pl.BlockSpec((up_rows, w_up.shape[1]), slab),
            pl.BlockSpec((down_rows, w_down.shape[1]), slab),
        ],
        out_shape=[
            jax.ShapeDtypeStruct((m, CONV_WIDTH), BF16),
            jax.ShapeDtypeStruct((m, HGRN_WIDTH), BF16),
            jax.ShapeDtypeStruct(w_up.shape, BF16),
            jax.ShapeDtypeStruct(w_down.shape, BF16),
        ],
        scratch_shapes=[pltpu.VMEM((HGRN_HEADS, HEAD_DIM, HEAD_DIM), F32),
                        pltpu.VMEM((HGRN_HEADS, CHUNK, LANES), F32),
                        pltpu.VMEM((CONV_GROUPS, CONV_HALO + TT_HGRN, LANES), F32)],
        compiler_params=pltpu.CompilerParams(
            dimension_semantics=("arbitrary", "arbitrary"),
            vmem_limit_bytes=VMEM_LIMIT),
        name="mixer",
    )(proj, proj, proj, proj, proj, proj, conv_w, conv_b, conv_ng, conv_nb, lb_logits.astype(F32), gn,
      jnp.asarray(msk), jnp.asarray(tri, dtype=BF16), jnp.asarray(eye), w_up, w_down)


def _outproj_kernel(u_ref, o_ref, x_ref, g0_ref, b0_ref, w_ref, g1_ref, b1_ref, h_ref, hb_ref, mix_ref):
    s = pl.program_id(0)

    @pl.when(s == 0)
    def _():
        mix_ref[...] = jnp.zeros(mix_ref.shape, F32)

    def step(slot_a, slot_b):
        mix_ref[slot_a] = (
            jnp.dot(u_ref[...], w_ref[pl.ds(0, CONV_WIDTH), :], preferred_element_type=F32)
            + jnp.dot(o_ref[...], w_ref[pl.ds(CONV_WIDTH, HGRN_WIDTH), :], preferred_element_type=F32))
        g0 = g0_ref[...]
        b0 = b0_ref[...]
        g1 = g1_ref[...]
        b1 = b1_ref[...]
        for r in range(TM_OUT // LN_ROWS_INLINE):
            rows = pl.ds(r * LN_ROWS_INLINE, LN_ROWS_INLINE)
            h0 = _layer_norm(x_ref[rows, :], g0, b0, LN_EPS)
            y = ALPHA * h0 + mix_ref[slot_b, rows, :]
            h1 = _layer_norm(y, g1, b1, LN_EPS)
            h_ref[rows, :] = h1
            hb_ref[rows, :] = h1.astype(BF16)

    @pl.when(s % 2 == 0)
    def _():
        step(0, 1)

    @pl.when(s % 2 == 1)
    def _():
        step(1, 0)


def _outproj(u, o, x2, g0, b0, w_bf16, g1, b1):
    m = x2.shape[0]
    n_tiles = m // TM_OUT
    row_a = lambda s: (jnp.minimum(s, n_tiles - 1), 0)
    row = lambda s: (jnp.maximum(s - 1, 0), 0)
    const = lambda s: (0, 0)
    return pl.pallas_call(
        _outproj_kernel,
        grid=(n_tiles + 1,),
        in_specs=[
            pl.BlockSpec((TM_OUT, CONV_WIDTH), row_a),
            pl.BlockSpec((TM_OUT, HGRN_WIDTH), row_a),
            pl.BlockSpec((TM_OUT, D_MODEL), row),
            pl.BlockSpec((1, D_MODEL), const),
            pl.BlockSpec((1, D_MODEL), const),
            pl.BlockSpec((D_MODEL, D_MODEL), const),
            pl.BlockSpec((1, D_MODEL), const),
            pl.BlockSpec((1, D_MODEL), const),
        ],
        out_specs=[pl.BlockSpec((TM_OUT, D_MODEL), row), pl.BlockSpec((TM_OUT, D_MODEL), row)],
        out_shape=[jax.ShapeDtypeStruct((m, D_MODEL), F32), jax.ShapeDtypeStruct((m, D_MODEL), BF16)],
        scratch_shapes=[pltpu.VMEM((2, TM_OUT, D_MODEL), F32)],
        compiler_params=pltpu.CompilerParams(
            dimension_semantics=("arbitrary",),
            vmem_limit_bytes=VMEM_LIMIT),
        name="outproj_ln1",
    )(u, o, x2, g0, b0, w_bf16, g1, b1)


def _ffn_kernel(tiles_per_seq, nf, hb_ref, hres_ref, wg_ref, wv_ref, cw_ref, cb_ref, wd_ref, g2_ref,
                b2_ref, o_ref, acc_ref, g_ref, v_ref, tail_ref):
    s = pl.program_id(0)
    tm = TM_FFN
    sb = jnp.maximum(s - 1, 0)
    i_b = sb // nf
    f_b = sb % nf

    @pl.when(s == 0)
    def _():
        acc_ref[...] = jnp.zeros(acc_ref.shape, F32)
        g_ref[...] = jnp.zeros(g_ref.shape, F32)
        v_ref[...] = jnp.zeros(v_ref.shape, F32)
        tail_ref[...] = jnp.zeros(tail_ref.shape, F32)

    def step(slot_a, slot_b):
        hb = hb_ref[...]
        g_ref[slot_a, pl.ds(SUBLANES, tm), :] = jnp.dot(hb, wg_ref[...], preferred_element_type=F32)
        v_ref[slot_a] = jnp.dot(hb, wv_ref[...], preferred_element_type=F32)

        seq_start = (i_b % tiles_per_seq) == 0
        g_ref[slot_b, pl.ds(0, SUBLANES), :] = jnp.where(seq_start, 0.0, tail_ref[f_b])
        tail_ref[f_b] = g_ref[slot_b, pl.ds(tm, SUBLANES), :]
        conv = (cb_ref[...]
                + cw_ref[pl.ds(2, 1), :] * g_ref[slot_b, pl.ds(SUBLANES, tm), :]
                + cw_ref[pl.ds(1, 1), :] * g_ref[slot_b, pl.ds(SUBLANES - 1, tm), :]
                + cw_ref[pl.ds(0, 1), :] * g_ref[slot_b, pl.ds(SUBLANES - 2, tm), :])
        act = (conv * _sigmoid(conv) * v_ref[slot_b]).astype(BF16)
        acc_ref[...] += jnp.dot(act, wd_ref[...], preferred_element_type=F32)

    @pl.when(s % 2 == 0)
    def _():
        step(0, 1)

    @pl.when(s % 2 == 1)
    def _():
        step(1, 0)

    @pl.when(jnp.logical_and(s > 0, f_b == nf - 1))
    def _():
        g2 = g2_ref[...]
        b2 = b2_ref[...]

        def body(r, c):
            r0 = pl.multiple_of(r * LN_ROWS, LN_ROWS)
            y = ALPHA * hres_ref[pl.ds(r0, LN_ROWS), :] + acc_ref[pl.ds(r0, LN_ROWS), :]
            o_ref[pl.ds(r0, LN_ROWS), :] = _layer_norm(y, g2, b2, LN_EPS)
            acc_ref[pl.ds(r0, LN_ROWS), :] = jnp.zeros((LN_ROWS, D_MODEL), F32)
            return c

        lax.fori_loop(0, tm // LN_ROWS, body, 0)


def _ffn(h1, h1_bf16, w_up_bf16, cw, cb, w_down_bf16, g2, b2, seq):
    m = h1.shape[0]
    nf = D_FF // TF_FFN
    n_steps = (m // TM_FFN) * nf
    a_step = lambda s: jnp.minimum(s, n_steps - 1)
    b_step = lambda s: jnp.maximum(s - 1, 0)
    return pl.pallas_call(
        functools.partial(_ffn_kernel, seq // TM_FFN, nf),
        grid=(n_steps + 1,),
        in_specs=[
            pl.BlockSpec((TM_FFN, D_MODEL), lambda s: (a_step(s) // nf, 0)),
            pl.BlockSpec((TM_FFN, D_MODEL), lambda s: (b_step(s) // nf, 0)),
            pl.BlockSpec((D_MODEL, TF_FFN), lambda s: (0, a_step(s) % nf)),
            pl.BlockSpec((D_MODEL, TF_FFN), lambda s: (0, nf + a_step(s) % nf)),
            pl.BlockSpec((FFN_KERNEL, TF_FFN), lambda s: (0, b_step(s) % nf)),
            pl.BlockSpec((1, TF_FFN), lambda s: (0, b_step(s) % nf)),
            pl.BlockSpec((TF_FFN, D_MODEL), lambda s: (b_step(s) % nf, 0)),
            pl.BlockSpec((1, D_MODEL), lambda s: (0, 0)),
            pl.BlockSpec((1, D_MODEL), lambda s: (0, 0)),
        ],
        out_specs=pl.BlockSpec((TM_FFN, D_MODEL), lambda s: (b_step(s) // nf, 0)),
        out_shape=jax.ShapeDtypeStruct((m, D_MODEL), F32),
        scratch_shapes=[
            pltpu.VMEM((TM_FFN, D_MODEL), F32),
            pltpu.VMEM((2, SUBLANES + TM_FFN, TF_FFN), F32),
            pltpu.VMEM((2, TM_FFN, TF_FFN), F32),
            pltpu.VMEM((nf, SUBLANES, TF_FFN), F32),
        ],
        compiler_params=pltpu.CompilerParams(
            dimension_semantics=("arbitrary",),
            vmem_limit_bytes=VMEM_LIMIT),
        name="conv_ffn_ln2",
    )(h1_bf16, h1, w_up_bf16, w_up_bf16, cw, cb, w_down_bf16, g2, b2)


def kernel(x, emb_ln_g, emb_ln_b, w_in, conv_w, conv_b, conv_norm_g, conv_norm_b, lb_logits,
           hgrn_norm_g, w_out, ln1_g, ln1_b, w_ffn_up, ffn_conv_w, ffn_conv_b, w_ffn_down,
           ln2_g, ln2_b):
    batch, seq, d = x.shape
    assert d == D_MODEL and w_in.shape[0] == DEPTH == 1
    assert seq % TT_HGRN == 0 and seq % TM_FFN == 0
    x2 = x.reshape(batch * seq, d)
    row = lambda a: a.reshape(1, -1).astype(F32)

    proj = _inproj(x2, row(emb_ln_g), row(emb_ln_b), w_in[0])
    u, o, w_up_bf16, w_down_bf16 = _mixer(
        proj, conv_w[0], row(conv_b[0]), row(conv_norm_g[0]), row(conv_norm_b[0]), lb_logits,
        row(hgrn_norm_g[0]), w_ffn_up[0], w_ffn_down[0], batch, seq)
    h1, h1_bf16 = _outproj(u, o, x2, row(emb_ln_g), row(emb_ln_b), w_out[0].astype(BF16),
                           row(ln1_g[0]), row(ln1_b[0]))
    out = _ffn(h1, h1_bf16, w_up_bf16, ffn_conv_w[0], row(ffn_conv_b[0]),
               w_down_bf16, row(ln2_g[0]), row(ln2_b[0]), seq)
    return out.reshape(batch, seq, d)
```

```python
import functools

import jax
import jax.numpy as jnp
import numpy as np
from jax import lax
from jax.experimental import pallas as pl
from jax.experimental.pallas import tpu as pltpu

F32 = jnp.float32
BF16 = jnp.bfloat16

D_MODEL = 2048
CONV_WIDTH = 1024
CONV_GROUPS = 8
CONV_KERNEL = 31
HGRN_WIDTH = 1024
HGRN_HEADS = 8
HEAD_DIM = 128
IN_PROJ_DIM = 2 * CONV_WIDTH + 4 * HGRN_WIDTH
D_FF = 5632
FFN_KERNEL = 3
LN_EPS = 1e-5
RMS_EPS = 1e-6
DEPTH = 1
ALPHA = (2.0 * DEPTH) ** 0.25
LOG2E = 1.4426950408889634

LANES = 128
SUBLANES = 8
VMEM_LIMIT = 56 * 1024 * 1024

TM_IN, TN_IN = 1024, 1024
IN_PIECE_ROWS = 256
CONV_HALO = 32
TT_HGRN = 512
CHUNK = 128
N_LEVELS = 7
CUM_ROWS = 256
CONV_ROWS = 64
N_SLABS = CHUNK // SUBLANES
TM_OUT = 512
TM_FFN, TF_FFN = 512, 512
LN_ROWS = 64
LN_ROWS_INLINE = 16

_NT = (((1,), (1,)), ((), ()))
_TN = (((0,), (0,)), ((), ()))


def _layer_norm(x, g, b, eps):
    mu = jnp.mean(x, axis=-1, keepdims=True)
    xc = x - mu
    var = jnp.mean(xc * xc, axis=-1, keepdims=True)
    return xc * lax.rsqrt(var + eps) * g + b


def _sigmoid(x):
    return 1.0 / (1.0 + jnp.exp2(x * (-LOG2E)))


def _inproj_kernel(x_ref, g_ref, b_ref, w_ref, o_ref, xn_ref):
    @pl.when(pl.program_id(1) == 0)
    def _():
        g = g_ref[...]
        b = b_ref[...]
        wb = w_ref[...].astype(BF16)
        for p in range(TM_IN // IN_PIECE_ROWS):
            for r in range(IN_PIECE_ROWS // LN_ROWS_INLINE):
                rows = pl.ds(p * IN_PIECE_ROWS + r * LN_ROWS_INLINE, LN_ROWS_INLINE)
                xn_ref[rows, :] = _layer_norm(x_ref[rows, :], g, b, LN_EPS).astype(BF16)
            prows = pl.ds(p * IN_PIECE_ROWS, IN_PIECE_ROWS)
            o_ref[prows, :] = jnp.dot(xn_ref[prows, :], wb, preferred_element_type=F32).astype(BF16)

    @pl.when(pl.program_id(1) != 0)
    def _():
        o_ref[...] = jnp.dot(xn_ref[...], w_ref[...].astype(BF16),
                             preferred_element_type=F32).astype(BF16)


def _inproj(x2, g, b, w_f32):
    m = x2.shape[0]
    return pl.pallas_call(
        _inproj_kernel,
        grid=(m // TM_IN, IN_PROJ_DIM // TN_IN),
        in_specs=[
            pl.BlockSpec((TM_IN, D_MODEL), lambda i, j: (i, 0)),
            pl.BlockSpec((1, D_MODEL), lambda i, j: (0, 0)),
            pl.BlockSpec((1, D_MODEL), lambda i, j: (0, 0)),
            pl.BlockSpec((D_MODEL, TN_IN), lambda i, j: (0, j)),
        ],
        out_specs=pl.BlockSpec((TM_IN, TN_IN), lambda i, j: (i, j)),
        out_shape=jax.ShapeDtypeStruct((m, IN_PROJ_DIM), BF16),
        scratch_shapes=[pltpu.VMEM((TM_IN, D_MODEL), BF16)],
        compiler_params=pltpu.CompilerParams(
            dimension_semantics=("arbitrary", "arbitrary"),
            vmem_limit_bytes=VMEM_LIMIT),
        name="ln_inproj",
    )(x2, g, b, w_f32)


def _hgrn_constants():
    t = np.arange(CHUNK)
    msk = np.zeros((N_LEVELS, CHUNK, CHUNK), np.float32)
    for l in range(N_LEVELS):
        m = 1 << l
        right = (t % (2 * m)) >= m
        same = (t[:, None] // (2 * m)) == (t[None, :] // (2 * m))
        msk[l] = (same & right[:, None] & (~right)[None, :]).astype(np.float32)
    r = np.arange(CUM_ROWS)
    tri = ((r[:, None] >= r[None, :]) & (r[:, None] // CHUNK == r[None, :] // CHUNK)).astype(np.float32)
    eye = np.eye(CHUNK, dtype=np.float32)
    return msk, tri, eye


def _hgrn_chunk_head(qr, vr, ogr, fg, b, b_ref, b_row0, cols, gn, st_ref, msk_ref, eye, sub, right_lo,
                     sgn_lo):
    kk = 1.0 - fg
    qh = qr * _sigmoid(qr)
    v = vr

    def slabs(x):
        return [x[i * SUBLANES:(i + 1) * SUBLANES] for i in range(N_SLABS)]

    def row(r):
        return jnp.broadcast_to(b_ref[pl.ds(b_row0 + r, 1), cols], (SUBLANES, LANES))

    bs, qs, ks, fs = slabs(b), slabs(qh), slabs(kk), slabs(fg)
    sc = slabs(eye * jnp.sum(qh * kk, axis=-1, keepdims=True))
    for l in range(N_LEVELS):
        m = 1 << l
        zs = []
        for i in range(N_SLABS):
            base = i * SUBLANES
            if m >= SUBLANES:
                bnd = row(base - base % (2 * m) + m - 1)
                if base % (2 * m) >= m:
                    zs.append(qs[i] * jnp.exp2(bs[i] - bnd))
                else:
                    zs.append(ks[i] * jnp.exp2(bnd - bs[i]))
            elif l == 0:
                zs.append(jnp.where(right_lo[0], qs[i] * fs[i], ks[i]))
            else:
                bnd = row(base + m - 1)
                for pp in range(1, SUBLANES // (2 * m)):
                    bnd = jnp.where(sub >= pp * 2 * m, row(base + pp * 2 * m + m - 1), bnd)
                e = (bs[i] - bnd) * sgn_lo[l]
                zs.append(jnp.where(right_lo[l], qs[i], ks[i]) * jnp.exp2(e))
        z = jnp.concatenate(zs, axis=0)
        s_l = jnp.dot(z.astype(BF16), z.T.astype(BF16), preferred_element_type=F32)
        for i in range(N_SLABS):
            base = i * SUBLANES
            if m >= SUBLANES and base % (2 * m) < m:
                continue
            sc[i] = sc[i] + msk_ref[l, pl.ds(base, SUBLANES), :] * s_l[base:base + SUBLANES]
    scores = jnp.concatenate(sc, axis=0)
    o_intra = jnp.dot(scores.astype(BF16), v, preferred_element_type=F32)

    st = st_ref[...]
    q_in = (qh * jnp.exp2(b)).astype(BF16)
    o_inter = lax.dot_general(q_in, st.astype(BF16), _NT, preferred_element_type=F32)
    b_last = b_ref[pl.ds(b_row0 + CHUNK - 1, 1), cols]
    k_up = (kk * jnp.exp2(b_last - b)).astype(BF16)
    st_ref[...] = st * jnp.exp2(b_last) + lax.dot_general(v, k_up, _TN, preferred_element_type=F32)

    o = o_intra + o_inter
    o = o * lax.rsqrt(jnp.mean(o * o, axis=-1, keepdims=True) + RMS_EPS) * gn
    return (o * (ogr * _sigmoid(ogr))).astype(BF16)


def _conv_rows_group(ubuf_ref, r0, cols, w_ref, cb, ng, nb):
    base = r0 + (CONV_HALO - (CONV_KERNEL - 1))
    acc = jnp.broadcast_to(cb, (CONV_ROWS, LANES))
    for r in range(SUBLANES):
        taps = list(range(r, CONV_KERNEL, SUBLANES))
        xr = ubuf_ref[pl.ds(base + r, CONV_ROWS + (len(taps) - 1) * SUBLANES), :]
        for a, k in enumerate(taps):
            acc = acc + w_ref[pl.ds(k, 1), cols] * xr[a * SUBLANES:a * SUBLANES + CONV_ROWS, :]
    y = _layer_norm(acc, ng, nb, LN_EPS)
    return (y * _sigmoid(y)).astype(BF16)


def _mixer_kernel(a_ref, gate_ref, q_ref, f_ref, i_ref, og_ref, cw_ref, ccb_ref, cng_ref, cnb_ref,
                  lbl_ref, gn_ref, msk_ref, tri_ref, eye_ref, wu_ref, wd_ref,
                  u_ref, o_ref, wub_ref, wdb_ref, st_ref, b_ref, fg_ref, ubuf_ref):
    tt = TT_HGRN

    @pl.when(pl.program_id(1) == 0)
    def _():
        st_ref[...] = jnp.zeros((HGRN_HEADS, HEAD_DIM, HEAD_DIM), F32)
        ubuf_ref[:, pl.ds(0, CONV_HALO), :] = jnp.zeros((CONV_GROUPS, CONV_HALO, LANES), F32)

    @pl.when(pl.program_id(1) != 0)
    def _():
        ubuf_ref[:, pl.ds(0, CONV_HALO), :] = ubuf_ref[:, pl.ds(tt, CONV_HALO), :]

    wub_ref[...] = wu_ref[...].astype(BF16)
    wdb_ref[...] = wd_ref[...].astype(BF16)

    rows = [lbl_ref[pl.ds(r, 1), :] for r in range(DEPTH + 1)]
    mx = functools.reduce(jnp.maximum, rows)
    ex = [jnp.exp(r - mx) for r in rows]
    lb_all = ex[0] / functools.reduce(lambda a, c: a + c, ex)
    gn_all = gn_ref[...]

    tri = tri_ref[...]
    for rb in range(tt // CUM_ROWS):
        crows = pl.ds(rb * CUM_ROWS, CUM_ROWS)
        fg = lb_all + (1.0 - lb_all) * _sigmoid(f_ref[crows, :].astype(F32))
        g2 = jnp.log(fg) * LOG2E
        g_hi = g2.astype(BF16)
        g_lo = (g2 - g_hi.astype(F32)).astype(BF16)
        b_ref[crows, :] = (jnp.dot(tri, g_hi, preferred_element_type=F32)
                           + jnp.dot(tri, g_lo, preferred_element_type=F32))
        fg_ref[crows, :] = fg

    sub = lax.broadcasted_iota(jnp.int32, (SUBLANES, LANES), 0)
    right_lo = [(sub % (2 << l)) >= (1 << l) for l in range(3)]
    sgn_lo = [jnp.where(r, 1.0, -1.0).astype(F32) for r in right_lo]
    eye = eye_ref[...]
    ccb_all = ccb_ref[...]
    cng_all = cng_ref[...]
    cnb_all = cnb_ref[...]

    for c in range(tt // CHUNK):
        r0 = c * CHUNK
        rws = pl.ds(r0, CHUNK)
        for h in range(HGRN_HEADS):
            cols = pl.ds(h * HEAD_DIM, HEAD_DIM)
            lane = slice(h * HEAD_DIM, (h + 1) * HEAD_DIM)
            ubuf_ref[h, pl.ds(CONV_HALO + r0, CHUNK), :] = (
                a_ref[rws, cols].astype(F32) * _sigmoid(gate_ref[rws, cols].astype(F32)))
            o_ref[rws, cols] = _hgrn_chunk_head(
                q_ref[rws, cols].astype(F32), i_ref[rws, cols], og_ref[rws, cols].astype(F32),
                fg_ref[rws, cols], b_ref[rws, cols], b_ref, r0, cols, gn_all[:, lane], st_ref.at[h],
                msk_ref, eye, sub, right_lo, sgn_lo)
            for j in range(CHUNK // CONV_ROWS):
                rr = r0 + j * CONV_ROWS
                u_ref[pl.ds(rr, CONV_ROWS), cols] = _conv_rows_group(
                    ubuf_ref.at[h], rr, cols, cw_ref, ccb_all[:, lane], cng_all[:, lane],
                    cnb_all[:, lane])


def _mixer(proj, conv_w, conv_b, conv_ng, conv_nb, lb_logits, gn, w_up, w_down, batch, seq):
    assert CONV_GROUPS == HGRN_HEADS and CONV_WIDTH == HGRN_WIDTH
    m = proj.shape[0]
    nt = seq // TT_HGRN
    n_steps = batch * nt
    msk, tri, eye = _hgrn_constants()
    col0 = 2 * CONV_WIDTH // HGRN_WIDTH
    up_rows = w_up.shape[0] // n_steps
    down_rows = w_down.shape[0] // n_steps
    assert up_rows * n_steps == w_up.shape[0] and down_rows * n_steps == w_down.shape[0]
    assert up_rows % 16 == 0 and down_rows % 16 == 0

    def col(n):
        return lambda b, j: (b * nt + j, n)

    slab = lambda b, j: (b * nt + j, 0)
    const2 = lambda b, j: (0, 0)
    return pl.pallas_call(
        _mixer_kernel,
        grid=(batch, nt),
        in_specs=[
            pl.BlockSpec((TT_HGRN, CONV_WIDTH), col(0)),
            pl.BlockSpec((TT_HGRN, CONV_WIDTH), col(1)),
            pl.BlockSpec((TT_HGRN, HGRN_WIDTH), col(col0)),
            pl.BlockSpec((TT_HGRN, HGRN_WIDTH), col(col0 + 1)),
            pl.BlockSpec((TT_HGRN, HGRN_WIDTH), col(col0 + 2)),
            pl.BlockSpec((TT_HGRN, HGRN_WIDTH), col(col0 + 3)),
            pl.BlockSpec((CONV_KERNEL, CONV_WIDTH), const2),
            pl.BlockSpec((1, CONV_WIDTH), const2),
            pl.BlockSpec((1, CONV_WIDTH), const2),
            pl.BlockSpec((1, CONV_WIDTH), const2),
            pl.BlockSpec((DEPTH + 1, HGRN_WIDTH), lambda b, j: (0, 0)),
            pl.BlockSpec((1, HGRN_WIDTH), lambda b, j: (0, 0)),
            pl.BlockSpec((N_LEVELS, CHUNK, CHUNK), lambda b, j: (0, 0, 0)),
            pl.BlockSpec((CUM_ROWS, CUM_ROWS), lambda b, j: (0, 0)),
            pl.BlockSpec((CHUNK, CHUNK), lambda b, j: (0, 0)),
            pl.BlockSpec((up_rows, w_up.shape[1]), slab),
            pl.BlockSpec((down_rows, w_down.shape[1]), slab),
        ],
        out_specs=[
            pl.BlockSpec((TT_HGRN, CONV_WIDTH), slab),
            pl.BlockSpec((TT_HGRN, HGRN_WIDTH), slab),
            pl.BlockSpec((up_rows, w_up.shape[1]), slab),
            pl.BlockSpec((down_rows, w_down.shape[1]), slab),
        ],
        out_shape=[
            jax.ShapeDtypeStruct((m, CONV_WIDTH), BF16),
            jax.ShapeDtypeStruct((m, HGRN_WIDTH), BF16),
            jax.ShapeDtypeStruct(w_up.shape, BF16),
            jax.ShapeDtypeStruct(w_down.shape, BF16),
        ],
        scratch_shapes=[pltpu.VMEM((HGRN_HEADS, HEAD_DIM, HEAD_DIM), F32),
                        pltpu.VMEM((TT_HGRN, HGRN_WIDTH), F32),
                        pltpu.VMEM((TT_HGRN, HGRN_WIDTH), F32),
                        pltpu.VMEM((CONV_GROUPS, CONV_HALO + TT_HGRN, LANES), F32)],
        compiler_params=pltpu.CompilerParams(
            dimension_semantics=("arbitrary", "arbitrary"),
            vmem_limit_bytes=VMEM_LIMIT),
        name="mixer",
    )(proj, proj, proj, proj, proj, proj, conv_w, conv_b, conv_ng, conv_nb, lb_logits.astype(F32), gn,
      jnp.asarray(msk), jnp.asarray(tri, dtype=BF16), jnp.asarray(eye), w_up, w_down)


def _outproj_kernel(u_ref, o_ref, x_ref, g0_ref, b0_ref, w_ref, g1_ref, b1_ref, h_ref, hb_ref, mix_ref):
    s = pl.program_id(0)

    @pl.when(s == 0)
    def _():
        mix_ref[...] = jnp.zeros(mix_ref.shape, F32)

    def step(slot_a, slot_b):
        mix_ref[slot_a] = (
            jnp.dot(u_ref[...], w_ref[pl.ds(0, CONV_WIDTH), :], preferred_element_type=F32)
            + jnp.dot(o_ref[...], w_ref[pl.ds(CONV_WIDTH, HGRN_WIDTH), :], preferred_element_type=F32))
        g0 = g0_ref[...]
        b0 = b0_ref[...]
        g1 = g1_ref[...]
        b1 = b1_ref[...]
        for r in range(TM_OUT // LN_ROWS_INLINE):
            rows = pl.ds(r * LN_ROWS_INLINE, LN_ROWS_INLINE)
            h0 = _layer_norm(x_ref[rows, :], g0, b0, LN_EPS)
            y = ALPHA * h0 + mix_ref[slot_b, rows, :]
            h1 = _layer_norm(y, g1, b1, LN_EPS)
            h_ref[rows, :] = h1
            hb_ref[rows, :] = h1.astype(BF16)

    @pl.when(s % 2 == 0)
    def _():
        step(0, 1)

    @pl.when(s % 2 == 1)
    def _():
        step(1, 0)


def _outproj(u, o, x2, g0, b0, w_bf16, g1, b1):
    m = x2.shape[0]
    n_tiles = m // TM_OUT
    row_a = lambda s: (jnp.minimum(s, n_tiles - 1), 0)
    row = lambda s: (jnp.maximum(s - 1, 0), 0)
    const = lambda s: (0, 0)
    return pl.pallas_call(
        _outproj_kernel,
        grid=(n_tiles + 1,),
        in_specs=[
            pl.BlockSpec((TM_OUT, CONV_WIDTH), row_a),
            pl.BlockSpec((TM_OUT, HGRN_WIDTH), row_a),
            pl.BlockSpec((TM_OUT, D_MODEL), row),
            pl.BlockSpec((1, D_MODEL), const),
            pl.BlockSpec((1, D_MODEL), const),
            pl.BlockSpec((D_MODEL, D_MODEL), const),
            pl.BlockSpec((1, D_MODEL), const),
            pl.BlockSpec((1, D_MODEL), const),
        ],
        out_specs=[pl.BlockSpec((TM_OUT, D_MODEL), row), pl.BlockSpec((TM_OUT, D_MODEL), row)],
        out_shape=[jax.ShapeDtypeStruct((m, D_MODEL), F32), jax.ShapeDtypeStruct((m, D_MODEL), BF16)],
        scratch_shapes=[pltpu.VMEM((2, TM_OUT, D_MODEL), F32)],
        compiler_params=pltpu.CompilerParams(
            dimension_semantics=("arbitrary",),
            vmem_limit_bytes=VMEM_LIMIT),
        name="outproj_ln1",
    )(u, o, x2, g0, b0, w_bf16, g1, b1)


def _ffn_kernel(tiles_per_seq, nf, hb_ref, hres_ref, wg_ref, wv_ref, cw_ref, cb_ref, wd_ref, g2_ref,
                b2_ref, o_ref, acc_ref, g_ref, v_ref, tail_ref):
    s = pl.program_id(0)
    tm = TM_FFN
    sb = jnp.maximum(s - 1, 0)
    i_b = sb // nf
    f_b = sb % nf

    @pl.when(s == 0)
    def _():
        acc_ref[...] = jnp.zeros(acc_ref.shape, F32)
        g_ref[...] = jnp.zeros(g_ref.shape, F32)
        v_ref[...] = jnp.zeros(v_ref.shape, F32)
        tail_ref[...] = jnp.zeros(tail_ref.shape, F32)

    def step(slot_a, slot_b):
        hb = hb_ref[...]
        g_ref[slot_a, pl.ds(SUBLANES, tm), :] = jnp.dot(hb, wg_ref[...], preferred_element_type=F32)
        v_ref[slot_a] = jnp.dot(hb, wv_ref[...], preferred_element_type=F32)

        seq_start = (i_b % tiles_per_seq) == 0
        g_ref[slot_b, pl.ds(0, SUBLANES), :] = jnp.where(seq_start, 0.0, tail_ref[f_b])
        tail_ref[f_b] = g_ref[slot_b, pl.ds(tm, SUBLANES), :]
        conv = (cb_ref[...]
                + cw_ref[pl.ds(2, 1), :] * g_ref[slot_b, pl.ds(SUBLANES, tm), :]
                + cw_ref[pl.ds(1, 1), :] * g_ref[slot_b, pl.ds(SUBLANES - 1, tm), :]
                + cw_ref[pl.ds(0, 1), :] * g_ref[slot_b, pl.ds(SUBLANES - 2, tm), :])
        act = (conv * _sigmoid(conv) * v_ref[slot_b]).astype(BF16)
        acc_ref[...] += jnp.dot(act, wd_ref[...], preferred_element_type=F32)

    @pl.when(s % 2 == 0)
    def _():
        step(0, 1)

    @pl.when(s % 2 == 1)
    def _():
        step(1, 0)

    @pl.when(jnp.logical_and(s > 0, f_b == nf - 1))
    def _():
        g2 = g2_ref[...]
        b2 = b2_ref[...]

        def body(r, c):
            r0 = pl.multiple_of(r * LN_ROWS, LN_ROWS)
            y = ALPHA * hres_ref[pl.ds(r0, LN_ROWS), :] + acc_ref[pl.ds(r0, LN_ROWS), :]
            o_ref[pl.ds(r0, LN_ROWS), :] = _layer_norm(y, g2, b2, LN_EPS)
            acc_ref[pl.ds(r0, LN_ROWS), :] = jnp.zeros((LN_ROWS, D_MODEL), F32)
            return c

        lax.fori_loop(0, tm // LN_ROWS, body, 0)


def _ffn(h1, h1_bf16, w_up_bf16, cw, cb, w_down_bf16, g2, b2, seq):
    m = h1.shape[0]
    nf = D_FF // TF_FFN
    n_steps = (m // TM_FFN) * nf
    a_step = lambda s: jnp.minimum(s, n_steps - 1)
    b_step = lambda s: jnp.maximum(s - 1, 0)
    return pl.pallas_call(
        functools.partial(_ffn_kernel, seq // TM_FFN, nf),
        grid=(n_steps + 1,),
        in_specs=[
            pl.BlockSpec((TM_FFN, D_MODEL), lambda s: (a_step(s) // nf, 0)),
            pl.BlockSpec((TM_FFN, D_MODEL), lambda s: (b_step(s) // nf, 0)),
            pl.BlockSpec((D_MODEL, TF_FFN), lambda s: (0, a_step(s) % nf)),
            pl.BlockSpec((D_MODEL, TF_FFN), lambda s: (0, nf + a_step(s) % nf)),
            pl.BlockSpec((FFN_KERNEL, TF_FFN), lambda s: (0, b_step(s) % nf)),
            pl.BlockSpec((1, TF_FFN), lambda s: (0, b_step(s) % nf)),
            pl.BlockSpec((TF_FFN, D_MODEL), lambda s: (b_step(s) % nf, 0)),
            pl.BlockSpec((1, D_MODEL), lambda s: (0, 0)),
            pl.BlockSpec((1, D_MODEL), lambda s: (0, 0)),
        ],
        out_specs=pl.BlockSpec((TM_FFN, D_MODEL), lambda s: (b_step(s) // nf, 0)),
        out_shape=jax.ShapeDtypeStruct((m, D_MODEL), F32),
        scratch_shapes=[
            pltpu.VMEM((TM_FFN, D_MODEL), F32),
            pltpu.VMEM((2, SUBLANES + TM_FFN, TF_FFN), F32),
            pltpu.VMEM((2, TM_FFN, TF_FFN), F32),
            pltpu.VMEM((nf, SUBLANES, TF_FFN), F32),
        ],
        compiler_params=pltpu.CompilerParams(
            dimension_semantics=("arbitrary",),
            vmem_limit_bytes=VMEM_LIMIT),
        name="conv_ffn_ln2",
    )(h1_bf16, h1, w_up_bf16, w_up_bf16, cw, cb, w_down_bf16, g2, b2)


def kernel(x, emb_ln_g, emb_ln_b, w_in, conv_w, conv_b, conv_norm_g, conv_norm_b, lb_logits,
           hgrn_norm_g, w_out, ln1_g, ln1_b, w_ffn_up, ffn_conv_w, ffn_conv_b, w_ffn_down,
           ln2_g, ln2_b):
    batch, seq, d = x.shape
    assert d == D_MODEL and w_in.shape[0] == DEPTH == 1
    assert seq % TT_HGRN == 0 and seq % TM_FFN == 0
    x2 = x.reshape(batch * seq, d)
    row = lambda a: a.reshape(1, -1).astype(F32)

    proj = _inproj(x2, row(emb_ln_g), row(emb_ln_b), w_in[0])
    u, o, w_up_bf16, w_down_bf16 = _mixer(
        proj, conv_w[0], row(conv_b[0]), row(conv_norm_g[0]), row(conv_norm_b[0]), lb_logits,
        row(hgrn_norm_g[0]), w_ffn_up[0], w_ffn_down[0], batch, seq)
    h1, h1_bf16 = _outproj(u, o, x2, row(emb_ln_g), row(emb_ln_b), w_out[0].astype(BF16),
                           row(ln1_g[0]), row(ln1_b[0]))
    out = _ffn(h1, h1_bf16, w_up_bf16, ffn_conv_w[0], row(ffn_conv_b[0]),
               w_down_bf16, row(ln2_g[0]), row(ln2_b[0]), seq)
    return out.reshape(batch, seq, d)
```

```python
import functools

import jax
import jax.numpy as jnp
import numpy as np
from jax import lax
from jax.experimental import pallas as pl
from jax.experimental.pallas import tpu as pltpu

F32 = jnp.float32
BF16 = jnp.bfloat16

D_MODEL = 2048
CONV_WIDTH = 1024
CONV_GROUPS = 8
CONV_KERNEL = 31
HGRN_WIDTH = 1024
HGRN_HEADS = 8
HEAD_DIM = 128
IN_PROJ_DIM = 2 * CONV_WIDTH + 4 * HGRN_WIDTH
D_FF = 5632
FFN_KERNEL = 3
LN_EPS = 1e-5
RMS_EPS = 1e-6
DEPTH = 1
ALPHA = (2.0 * DEPTH) ** 0.25
LOG2E = 1.4426950408889634

LANES = 128
SUBLANES = 8
VMEM_LIMIT = 56 * 1024 * 1024

TM_IN, TN_IN = 1024, 1024
IN_PIECE_ROWS = 256
CONV_HALO = 32
TT_HGRN = 256
OUT_PIECES = 8
CHUNK = 128
N_LEVELS = 7
CUM_ROWS = 256
CONV_ROWS = 64
N_SLABS = CHUNK // SUBLANES
TM_FFN, TF_FFN = 512, 512
FFN_K_COLS = 256
LN_ROWS = 64
LN_ROWS_INLINE = 16

_NT = (((1,), (1,)), ((), ()))
_TN = (((0,), (0,)), ((), ()))


def _layer_norm(x, g, b, eps):
    mu = jnp.mean(x, axis=-1, keepdims=True)
    xc = x - mu
    var = jnp.mean(xc * xc, axis=-1, keepdims=True)
    return xc * lax.rsqrt(var + eps) * g + b


def _sigmoid(x):
    return 1.0 / (1.0 + jnp.exp2(x * (-LOG2E)))


def _inproj_kernel(x_ref, g_ref, b_ref, w_ref, o_ref, xn_ref):
    @pl.when(pl.program_id(1) == 0)
    def _():
        g = g_ref[...]
        b = b_ref[...]
        wb = w_ref[...].astype(BF16)
        for p in range(TM_IN // IN_PIECE_ROWS):
            for r in range(IN_PIECE_ROWS // LN_ROWS_INLINE):
                rows = pl.ds(p * IN_PIECE_ROWS + r * LN_ROWS_INLINE, LN_ROWS_INLINE)
                xn_ref[rows, :] = _layer_norm(x_ref[rows, :], g, b, LN_EPS).astype(BF16)
            prows = pl.ds(p * IN_PIECE_ROWS, IN_PIECE_ROWS)
            o_ref[prows, :] = jnp.dot(xn_ref[prows, :], wb, preferred_element_type=F32).astype(BF16)

    @pl.when(pl.program_id(1) != 0)
    def _():
        o_ref[...] = jnp.dot(xn_ref[...], w_ref[...].astype(BF16),
                             preferred_element_type=F32).astype(BF16)


def _inproj(x2, g, b, w_f32):
    m = x2.shape[0]
    return pl.pallas_call(
        _inproj_kernel,
        grid=(m // TM_IN, IN_PROJ_DIM // TN_IN),
        in_specs=[
            pl.BlockSpec((TM_IN, D_MODEL), lambda i, j: (i, 0)),
            pl.BlockSpec((1, D_MODEL), lambda i, j: (0, 0)),
            pl.BlockSpec((1, D_MODEL), lambda i, j: (0, 0)),
            pl.BlockSpec((D_MODEL, TN_IN), lambda i, j: (0, j)),
        ],
        out_specs=pl.BlockSpec((TM_IN, TN_IN), lambda i, j: (i, j)),
        out_shape=jax.ShapeDtypeStruct((m, IN_PROJ_DIM), BF16),
        scratch_shapes=[pltpu.VMEM((TM_IN, D_MODEL), BF16)],
        compiler_params=pltpu.CompilerParams(
            dimension_semantics=("arbitrary", "arbitrary"),
            vmem_limit_bytes=VMEM_LIMIT),
        name="ln_inproj",
    )(x2, g, b, w_f32)


def _hgrn_constants():
    t = np.arange(CHUNK)
    msk = np.zeros((N_LEVELS, CHUNK, CHUNK), np.float32)
    for l in range(N_LEVELS):
        m = 1 << l
        right = (t % (2 * m)) >= m
        same = (t[:, None] // (2 * m)) == (t[None, :] // (2 * m))
        msk[l] = (same & right[:, None] & (~right)[None, :]).astype(np.float32)
    r = np.arange(CUM_ROWS)
    tri = ((r[:, None] >= r[None, :]) & (r[:, None] // CHUNK == r[None, :] // CHUNK)).astype(np.float32)
    eye = np.eye(CHUNK, dtype=np.float32)
    return msk, tri, eye


def _hgrn_chunk_head(qr, vr, ogr, fg, b, b_ref, b_row0, cols, gn, st_ref, msk_ref, eye, sub, right_lo,
                     sgn_lo):
    kk = 1.0 - fg
    qh = qr * _sigmoid(qr)
    v = vr

    def slabs(x):
        return [x[i * SUBLANES:(i + 1) * SUBLANES] for i in range(N_SLABS)]

    def row(r):
        return jnp.broadcast_to(b_ref[pl.ds(b_row0 + r, 1), cols], (SUBLANES, LANES))

    bs, qs, ks, fs = slabs(b), slabs(qh), slabs(kk), slabs(fg)
    sc = slabs(eye * jnp.sum(qh * kk, axis=-1, keepdims=True))
    for l in range(N_LEVELS):
        m = 1 << l
        zs = []
        for i in range(N_SLABS):
            base = i * SUBLANES
            if m >= SUBLANES:
                bnd = row(base - base % (2 * m) + m - 1)
                if base % (2 * m) >= m:
                    zs.append(qs[i] * jnp.exp2(bs[i] - bnd))
                else:
                    zs.append(ks[i] * jnp.exp2(bnd - bs[i]))
            elif l == 0:
                zs.append(jnp.where(right_lo[0], qs[i] * fs[i], ks[i]))
            else:
                bnd = row(base + m - 1)
                for pp in range(1, SUBLANES // (2 * m)):
                    bnd = jnp.where(sub >= pp * 2 * m, row(base + pp * 2 * m + m - 1), bnd)
                e = (bs[i] - bnd) * sgn_lo[l]
                zs.append(jnp.where(right_lo[l], qs[i], ks[i]) * jnp.exp2(e))
        z = jnp.concatenate(zs, axis=0)
        s_l = jnp.dot(z.astype(BF16), z.T.astype(BF16), preferred_element_type=F32)
        for i in range(N_SLABS):
            base = i * SUBLANES
            if m >= SUBLANES and base % (2 * m) < m:
                continue
            sc[i] = sc[i] + msk_ref[l, pl.ds(base, SUBLANES), :] * s_l[base:base + SUBLANES]
    scores = jnp.concatenate(sc, axis=0)
    o_intra = jnp.dot(scores.astype(BF16), v, preferred_element_type=F32)

    st = st_ref[...]
    q_in = (qh * jnp.exp2(b)).astype(BF16)
    o_inter = lax.dot_general(q_in, st.astype(BF16), _NT, preferred_element_type=F32)
    b_last = b_ref[pl.ds(b_row0 + CHUNK - 1, 1), cols]
    k_up = (kk * jnp.exp2(b_last - b)).astype(BF16)
    st_ref[...] = st * jnp.exp2(b_last) + lax.dot_general(v, k_up, _TN, preferred_element_type=F32)

    o = o_intra + o_inter
    o = o * lax.rsqrt(jnp.mean(o * o, axis=-1, keepdims=True) + RMS_EPS) * gn
    return (o * (ogr * _sigmoid(ogr))).astype(BF16)


def _conv_rows_group(ubuf_ref, r0, cols, w_ref, cb, ng, nb):
    base = r0 + (CONV_HALO - (CONV_KERNEL - 1))
    acc = jnp.broadcast_to(cb, (CONV_ROWS, LANES))
    for r in range(SUBLANES):
        taps = list(range(r, CONV_KERNEL, SUBLANES))
        xr = ubuf_ref[pl.ds(base + r, CONV_ROWS + (len(taps) - 1) * SUBLANES), :]
        for a, k in enumerate(taps):
            acc = acc + w_ref[pl.ds(k, 1), cols] * xr[a * SUBLANES:a * SUBLANES + CONV_ROWS, :]
    y = _layer_norm(acc, ng, nb, LN_EPS)
    return (y * _sigmoid(y)).astype(BF16)


def _mixer_kernel(n_tiles, tiles_per_seq,
                  a_ref, gate_ref, q_ref, f_ref, i_ref, og_ref, cw_ref, ccb_ref, cng_ref, cnb_ref,
                  lbl_ref, gn_ref, msk_ref, tri_ref, eye_ref, wu_ref, wd_ref,
                  x_ref, g0_ref, b0_ref, wout_ref, g1_ref, b1_ref,
                  h_ref, hb_ref, wub_ref, wdb_ref,
                  st_ref, b_ref, fg_ref, ubuf_ref, uo_ref, mix_ref):
    tt = TT_HGRN
    s = pl.program_id(0)
    seq_start = (jnp.minimum(s, n_tiles - 1) % tiles_per_seq) == 0
    slot_w = s % 2
    slot_r = 1 - slot_w

    @pl.when(s == 0)
    def _():
        uo_ref[...] = jnp.zeros(uo_ref.shape, BF16)

    @pl.when(seq_start)
    def _():
        st_ref[...] = jnp.zeros((HGRN_HEADS, HEAD_DIM, HEAD_DIM), F32)
        ubuf_ref[:, pl.ds(0, CONV_HALO), :] = jnp.zeros((CONV_GROUPS, CONV_HALO, LANES), F32)

    @pl.when(jnp.logical_not(seq_start))
    def _():
        ubuf_ref[:, pl.ds(0, CONV_HALO), :] = ubuf_ref[:, pl.ds(tt, CONV_HALO), :]

    wub_ref[...] = wu_ref[...].astype(BF16)
    wdb_ref[...] = wd_ref[...].astype(BF16)

    rows = [lbl_ref[pl.ds(r, 1), :] for r in range(DEPTH + 1)]
    mx = functools.reduce(jnp.maximum, rows)
    ex = [jnp.exp(r - mx) for r in rows]
    lb_all = ex[0] / functools.reduce(lambda a, c: a + c, ex)
    gn_all = gn_ref[...]

    tri = tri_ref[...]
    for rb in range(tt // CUM_ROWS):
        crows = pl.ds(rb * CUM_ROWS, CUM_ROWS)
        fg = lb_all + (1.0 - lb_all) * _sigmoid(f_ref[crows, :].astype(F32))
        g2 = jnp.log(fg) * LOG2E
        g_hi = g2.astype(BF16)
        g_lo = (g2 - g_hi.astype(F32)).astype(BF16)
        b_ref[crows, :] = (jnp.dot(tri, g_hi, preferred_element_type=F32)
                           + jnp.dot(tri, g_lo, preferred_element_type=F32))
        fg_ref[crows, :] = fg

    sub = lax.broadcasted_iota(jnp.int32, (SUBLANES, LANES), 0)
    right_lo = [(sub % (2 << l)) >= (1 << l) for l in range(3)]
    sgn_lo = [jnp.where(r, 1.0, -1.0).astype(F32) for r in right_lo]
    eye = eye_ref[...]
    ccb_all = ccb_ref[...]
    cng_all = cng_ref[...]
    cnb_all = cnb_ref[...]

    n_units = (tt // CHUNK) * HGRN_HEADS
    out_cols = D_MODEL // OUT_PIECES
    n_ln = tt // LN_ROWS_INLINE
    ln_per_unit = -(-n_ln // (n_units - OUT_PIECES))

    for c in range(tt // CHUNK):
        r0 = c * CHUNK
        rws = pl.ds(r0, CHUNK)
        for h in range(HGRN_HEADS):
            cols = pl.ds(h * HEAD_DIM, HEAD_DIM)
            lane = slice(h * HEAD_DIM, (h + 1) * HEAD_DIM)
            ubuf_ref[h, pl.ds(CONV_HALO + r0, CHUNK), :] = (
                a_ref[rws, cols].astype(F32) * _sigmoid(gate_ref[rws, cols].astype(F32)))
            uo_ref[slot_w, rws, pl.ds(CONV_WIDTH + h * HEAD_DIM, HEAD_DIM)] = _hgrn_chunk_head(
                q_ref[rws, cols].astype(F32), i_ref[rws, cols], og_ref[rws, cols].astype(F32),
                fg_ref[rws, cols], b_ref[rws, cols], b_ref, r0, cols, gn_all[:, lane], st_ref.at[h],
                msk_ref, eye, sub, right_lo, sgn_lo)
            for j in range(CHUNK // CONV_ROWS):
                rr = r0 + j * CONV_ROWS
                uo_ref[slot_w, pl.ds(rr, CONV_ROWS), cols] = _conv_rows_group(
                    ubuf_ref.at[h], rr, cols, cw_ref, ccb_all[:, lane], cng_all[:, lane],
                    cnb_all[:, lane])

            unit = c * HGRN_HEADS + h
            if unit < OUT_PIECES:
                pc = pl.ds(unit * out_cols, out_cols)
                mix_ref[:, pc] = jnp.dot(uo_ref[slot_r], wout_ref[:, pc], preferred_element_type=F32)
            else:
                for r in range((unit - OUT_PIECES) * ln_per_unit,
                               min((unit - OUT_PIECES + 1) * ln_per_unit, n_ln)):
                    rows = pl.ds(r * LN_ROWS_INLINE, LN_ROWS_INLINE)
                    h0 = _layer_norm(x_ref[rows, :], g0_ref[...], b0_ref[...], LN_EPS)
                    y = ALPHA * h0 + mix_ref[rows, :]
                    h1 = _layer_norm(y, g1_ref[...], b1_ref[...], LN_EPS)
                    h_ref[rows, :] = h1
                    hb_ref[rows, :] = h1.astype(BF16)


def _mixer(proj, conv_w, conv_b, conv_ng, conv_nb, lb_logits, gn, w_up, w_down,
           x2, g0, b0, w_out_bf16, g1, b1, batch, seq):
    assert CONV_GROUPS == HGRN_HEADS and CONV_WIDTH == HGRN_WIDTH
    m = proj.shape[0]
    nt = seq // TT_HGRN
    n_steps = batch * nt
    msk, tri, eye = _hgrn_constants()
    col0 = 2 * CONV_WIDTH // HGRN_WIDTH
    up_rows = w_up.shape[0] // n_steps
    down_rows = w_down.shape[0] // n_steps
    assert up_rows * n_steps == w_up.shape[0] and down_rows * n_steps == w_down.shape[0]
    assert up_rows % 16 == 0 and down_rows % 16 == 0
    cur = lambda s: jnp.minimum(s, n_steps - 1)
    prev = lambda s: jnp.maximum(s - 1, 0)

    def col(n):
        return lambda s: (cur(s), n)

    slab = lambda s: (cur(s), 0)
    prow = lambda s: (prev(s), 0)
    const2 = lambda s: (0, 0)
    return pl.pallas_call(
        functools.partial(_mixer_kernel, n_steps, nt),
        grid=(n_steps + 1,),
        in_specs=[
            pl.BlockSpec((TT_HGRN, CONV_WIDTH), col(0)),
            pl.BlockSpec((TT_HGRN, CONV_WIDTH), col(1)),
            pl.BlockSpec((TT_HGRN, HGRN_WIDTH), col(col0)),
            pl.BlockSpec((TT_HGRN, HGRN_WIDTH), col(col0 + 1)),
            pl.BlockSpec((TT_HGRN, HGRN_WIDTH), col(col0 + 2)),
            pl.BlockSpec((TT_HGRN, HGRN_WIDTH), col(col0 + 3)),
            pl.BlockSpec((CONV_KERNEL, CONV_WIDTH), const2),
            pl.BlockSpec((1, CONV_WIDTH), const2),
            pl.BlockSpec((1, CONV_WIDTH), const2),
            pl.BlockSpec((1, CONV_WIDTH), const2),
            pl.BlockSpec((DEPTH + 1, HGRN_WIDTH), const2),
            pl.BlockSpec((1, HGRN_WIDTH), const2),
            pl.BlockSpec((N_LEVELS, CHUNK, CHUNK), lambda s: (0, 0, 0)),
            pl.BlockSpec((CUM_ROWS, CUM_ROWS), const2),
            pl.BlockSpec((CHUNK, CHUNK), const2),
            pl.BlockSpec((up_rows, w_up.shape[1]), slab),
            pl.BlockSpec((down_rows, w_down.shape[1]), slab),
            pl.BlockSpec((TT_HGRN, D_MODEL), prow),
            pl.BlockSpec((1, D_MODEL), const2),
            pl.BlockSpec((1, D_MODEL), const2),
            pl.BlockSpec((D_MODEL, D_MODEL), const2, pipeline_mode=pl.Buffered(1)),
            pl.BlockSpec((1, D_MODEL), const2),
            pl.BlockSpec((1, D_MODEL), const2),
        ],
        out_specs=[
            pl.BlockSpec((TT_HGRN, D_MODEL), prow),
            pl.BlockSpec((TT_HGRN, D_MODEL), prow),
            pl.BlockSpec((up_rows, w_up.shape[1]), slab),
            pl.BlockSpec((down_rows, w_down.shape[1]), slab),
        ],
        out_shape=[
            jax.ShapeDtypeStruct((m, D_MODEL), F32),
            jax.ShapeDtypeStruct((m, D_MODEL), BF16),
            jax.ShapeDtypeStruct(w_up.shape, BF16),
            jax.ShapeDtypeStruct(w_down.shape, BF16),
        ],
        scratch_shapes=[pltpu.VMEM((HGRN_HEADS, HEAD_DIM, HEAD_DIM), F32),
                        pltpu.VMEM((TT_HGRN, HGRN_WIDTH), F32),
                        pltpu.VMEM((TT_HGRN, HGRN_WIDTH), F32),
                        pltpu.VMEM((CONV_GROUPS, CONV_HALO + TT_HGRN, LANES), F32),
                        pltpu.VMEM((2, TT_HGRN, D_MODEL), BF16),
                        pltpu.VMEM((TT_HGRN, D_MODEL), F32)],
        compiler_params=pltpu.CompilerParams(
            dimension_semantics=("arbitrary",),
            vmem_limit_bytes=VMEM_LIMIT),
        name="mixer",
    )(proj, proj, proj, proj, proj, proj, conv_w, conv_b, conv_ng, conv_nb, lb_logits.astype(F32), gn,
      jnp.asarray(msk), jnp.asarray(tri, dtype=BF16), jnp.asarray(eye), w_up, w_down,
      x2, g0, b0, w_out_bf16, g1, b1)


def _ffn_kernel(tiles_per_seq, nf, hb_ref, hres_ref, wg_ref, wv_ref, cw_ref, cb_ref, wd_ref, g2_ref,
                b2_ref, o_ref, acc_ref, g_ref, v_ref, tail_ref):
    s = pl.program_id(0)
    tm = TM_FFN
    sb = jnp.maximum(s - 1, 0)
    i_b = sb // nf
    f_b = sb % nf

    @pl.when(s == 0)
    def _():
        acc_ref[...] = jnp.zeros(acc_ref.shape, F32)
        g_ref[...] = jnp.zeros(g_ref.shape, F32)
        v_ref[...] = jnp.zeros(v_ref.shape, F32)
        tail_ref[...] = jnp.zeros(tail_ref.shape, F32)

    def step(slot_a, slot_b):
        hb = hb_ref[...]
        g_ref[slot_a, pl.ds(SUBLANES, tm), :] = jnp.dot(hb, wg_ref[...], preferred_element_type=F32)
        v_ref[slot_a] = jnp.dot(hb, wv_ref[...], preferred_element_type=F32)

        seq_start = (i_b % tiles_per_seq) == 0
        g_ref[slot_b, pl.ds(0, SUBLANES), :] = jnp.where(seq_start, 0.0, tail_ref[f_b])
        tail_ref[f_b] = g_ref[slot_b, pl.ds(tm, SUBLANES), :]
        part = None
        for kc in range(TF_FFN // FFN_K_COLS):
            cols = pl.ds(kc * FFN_K_COLS, FFN_K_COLS)
            conv = (cb_ref[:, cols]
                    + cw_ref[pl.ds(2, 1), cols] * g_ref[slot_b, pl.ds(SUBLANES, tm), cols]
                    + cw_ref[pl.ds(1, 1), cols] * g_ref[slot_b, pl.ds(SUBLANES - 1, tm), cols]
                    + cw_ref[pl.ds(0, 1), cols] * g_ref[slot_b, pl.ds(SUBLANES - 2, tm), cols])
            act = (conv * _sigmoid(conv) * v_ref[slot_b, :, cols]).astype(BF16)
            d = jnp.dot(act, wd_ref[cols, :], preferred_element_type=F32)
            part = d if part is None else part + d
        acc_ref[...] += part

    @pl.when(s % 2 == 0)
    def _():
        step(0, 1)

    @pl.when(s % 2 == 1)
    def _():
        step(1, 0)

    @pl.when(jnp.logical_and(s > 0, f_b == nf - 1))
    def _():
        g2 = g2_ref[...]
        b2 = b2_ref[...]

        def body(r, c):
            r0 = pl.multiple_of(r * LN_ROWS, LN_ROWS)
            y = ALPHA * hres_ref[pl.ds(r0, LN_ROWS), :] + acc_ref[pl.ds(r0, LN_ROWS), :]
            o_ref[pl.ds(r0, LN_ROWS), :] = _layer_norm(y, g2, b2, LN_EPS)
            acc_ref[pl.ds(r0, LN_ROWS), :] = jnp.zeros((LN_ROWS, D_MODEL), F32)
            return c

        lax.fori_loop(0, tm // LN_ROWS, body, 0)


def _ffn(h1, h1_bf16, w_up_bf16, cw, cb, w_down_bf16, g2, b2, seq):
    m = h1.shape[0]
    nf = D_FF // TF_FFN
    n_steps = (m // TM_FFN) * nf
    a_step = lambda s: jnp.minimum(s, n_steps - 1)
    b_step = lambda s: jnp.maximum(s - 1, 0)
    return pl.pallas_call(
        functools.partial(_ffn_kernel, seq // TM_FFN, nf),
        grid=(n_steps + 1,),
        in_specs=[
            pl.BlockSpec((TM_FFN, D_MODEL), lambda s: (a_step(s) // nf, 0)),
            pl.BlockSpec((TM_FFN, D_MODEL), lambda s: (b_step(s) // nf, 0)),
            pl.BlockSpec((D_MODEL, TF_FFN), lambda s: (0, a_step(s) % nf)),
            pl.BlockSpec((D_MODEL, TF_FFN), lambda s: (0, nf + a_step(s) % nf)),
            pl.BlockSpec((FFN_KERNEL, TF_FFN), lambda s: (0, b_step(s) % nf)),
            pl.BlockSpec((1, TF_FFN), lambda s: (0, b_step(s) % nf)),
            pl.BlockSpec((TF_FFN, D_MODEL), lambda s: (b_step(s) % nf, 0)),
            pl.BlockSpec((1, D_MODEL), lambda s: (0, 0)),
            pl.BlockSpec((1, D_MODEL), lambda s: (0, 0)),
        ],
        out_specs=pl.BlockSpec((TM_FFN, D_MODEL), lambda s: (b_step(s) // nf, 0)),
        out_shape=jax.ShapeDtypeStruct((m, D_MODEL), F32),
        scratch_shapes=[
            pltpu.VMEM((TM_FFN, D_MODEL), F32),
            pltpu.VMEM((2, SUBLANES + TM_FFN, TF_FFN), F32),
            pltpu.VMEM((2, TM_FFN, TF_FFN), F32),
            pltpu.VMEM((nf, SUBLANES, TF_FFN), F32),
        ],
        compiler_params=pltpu.CompilerParams(
            dimension_semantics=("arbitrary",),
            vmem_limit_bytes=VMEM_LIMIT),
        name="conv_ffn_ln2",
    )(h1_bf16, h1, w_up_bf16, w_up_bf16, cw, cb, w_down_bf16, g2, b2)


def kernel(x, emb_ln_g, emb_ln_b, w_in, conv_w, conv_b, conv_norm_g, conv_norm_b, lb_logits,
           hgrn_norm_g, w_out, ln1_g, ln1_b, w_ffn_up, ffn_conv_w, ffn_conv_b, w_ffn_down,
           ln2_g, ln2_b):
    batch, seq, d = x.shape
    assert d == D_MODEL and w_in.shape[0] == DEPTH == 1
    assert seq % TT_HGRN == 0 and seq % TM_FFN == 0
    x2 = x.reshape(batch * seq, d)
    row = lambda a: a.reshape(1, -1).astype(F32)

    proj = _inproj(x2, row(emb_ln_g), row(emb_ln_b), w_in[0])
    h1, h1_bf16, w_up_bf16, w_down_bf16 = _mixer(
        proj, conv_w[0], row(conv_b[0]), row(conv_norm_g[0]), row(conv_norm_b[0]), lb_logits,
        row(hgrn_norm_g[0]), w_ffn_up[0], w_ffn_down[0],
        x2, row(emb_ln_g), row(emb_ln_b), w_out[0].astype(BF16), row(ln1_g[0]), row(ln1_b[0]),
        batch, seq)
    out = _ffn(h1, h1_bf16, w_up_bf16, ffn_conv_w[0], row(ffn_conv_b[0]),
               w_down_bf16, row(ln2_g[0]), row(ln2_b[0]), seq)
    return out.reshape(batch, seq, d)
```

```python
import functools

import jax
import jax.numpy as jnp
import numpy as np
from jax import lax
from jax.experimental import pallas as pl
from jax.experimental.pallas import tpu as pltpu

F32 = jnp.float32
BF16 = jnp.bfloat16

D_MODEL = 2048
CONV_WIDTH = 1024
CONV_GROUPS = 8
CONV_KERNEL = 31
HGRN_WIDTH = 1024
HGRN_HEADS = 8
HEAD_DIM = 128
IN_PROJ_DIM = 2 * CONV_WIDTH + 4 * HGRN_WIDTH
D_FF = 5632
FFN_KERNEL = 3
LN_EPS = 1e-5
RMS_EPS = 1e-6
DEPTH = 1
ALPHA = (2.0 * DEPTH) ** 0.25
LOG2E = 1.4426950408889634

LANES = 128
SUBLANES = 8
VMEM_LIMIT = 56 * 1024 * 1024

TM_IN, TN_IN = 1024, 1024
IN_PIECE_ROWS = 256
CONV_HALO = 32
TT_HGRN = 256
OUT_PIECES = 8
CHUNK = 128
N_LEVELS = 7
CUM_ROWS = 256
CONV_ROWS = 64
N_SLABS = CHUNK // SUBLANES
TM_FFN, TF_FFN = 512, 512
FFN_K_COLS = 256
LN_ROWS = 64
LN_ROWS_INLINE = 16

_NT = (((1,), (1,)), ((), ()))
_TN = (((0,), (0,)), ((), ()))


def _layer_norm(x, g, b, eps):
    mu = jnp.mean(x, axis=-1, keepdims=True)
    xc = x - mu
    var = jnp.mean(xc * xc, axis=-1, keepdims=True)
    return xc * lax.rsqrt(var + eps) * g + b


def _sigmoid(x):
    return 1.0 / (1.0 + jnp.exp2(x * (-LOG2E)))


def _inproj_kernel(x_ref, g_ref, b_ref, w_ref, o_ref, xn_ref):
    @pl.when(pl.program_id(1) == 0)
    def _():
        g = g_ref[...]
        b = b_ref[...]
        wb = w_ref[...].astype(BF16)
        for p in range(TM_IN // IN_PIECE_ROWS):
            for r in range(IN_PIECE_ROWS // LN_ROWS_INLINE):
                rows = pl.ds(p * IN_PIECE_ROWS + r * LN_ROWS_INLINE, LN_ROWS_INLINE)
                xn_ref[rows, :] = _layer_norm(x_ref[rows, :], g, b, LN_EPS).astype(BF16)
            prows = pl.ds(p * IN_PIECE_ROWS, IN_PIECE_ROWS)
            o_ref[prows, :] = jnp.dot(xn_ref[prows, :], wb, preferred_element_type=F32).astype(BF16)

    @pl.when(pl.program_id(1) != 0)
    def _():
        o_ref[...] = jnp.dot(xn_ref[...], w_ref[...].astype(BF16),
                             preferred_element_type=F32).astype(BF16)


def _inproj(x2, g, b, w_f32):
    m = x2.shape[0]
    return pl.pallas_call(
        _inproj_kernel,
        grid=(m // TM_IN, IN_PROJ_DIM // TN_IN),
        in_specs=[
            pl.BlockSpec((TM_IN, D_MODEL), lambda i, j: (i, 0)),
            pl.BlockSpec((1, D_MODEL), lambda i, j: (0, 0)),
            pl.BlockSpec((1, D_MODEL), lambda i, j: (0, 0)),
            pl.BlockSpec((D_MODEL, TN_IN), lambda i, j: (0, j)),
        ],
        out_specs=pl.BlockSpec((TM_IN, TN_IN), lambda i, j: (i, j)),
        out_shape=jax.ShapeDtypeStruct((m, IN_PROJ_DIM), BF16),
        scratch_shapes=[pltpu.VMEM((TM_IN, D_MODEL), BF16)],
        compiler_params=pltpu.CompilerParams(
            dimension_semantics=("arbitrary", "arbitrary"),
            vmem_limit_bytes=VMEM_LIMIT),
        name="ln_inproj",
    )(x2, g, b, w_f32)


def _hgrn_constants():
    t = np.arange(CHUNK)
    msk = np.zeros((N_LEVELS, CHUNK, CHUNK), np.float32)
    for l in range(N_LEVELS):
        m = 1 << l
        right = (t % (2 * m)) >= m
        same = (t[:, None] // (2 * m)) == (t[None, :] // (2 * m))
        msk[l] = (same & right[:, None] & (~right)[None, :]).astype(np.float32)
    r = np.arange(CUM_ROWS)
    tri = ((r[:, None] >= r[None, :]) & (r[:, None] // CHUNK == r[None, :] // CHUNK)).astype(np.float32)
    eye = np.eye(CHUNK, dtype=np.float32)
    return msk, tri, eye


def _hgrn_chunk_head(qr, vr, ogr, fg, b, b_ref, b_row0, cols, gn, st_ref, msk_ref, eye, sub, right_lo,
                     sgn_lo):
    kk = 1.0 - fg
    qh = qr * _sigmoid(qr)
    v = vr

    def slabs(x):
        return [x[i * SUBLANES:(i + 1) * SUBLANES] for i in range(N_SLABS)]

    def row(r):
        return jnp.broadcast_to(b_ref[pl.ds(b_row0 + r, 1), cols], (SUBLANES, LANES))

    bs, qs, ks, fs = slabs(b), slabs(qh), slabs(kk), slabs(fg)
    sc = slabs(eye * jnp.sum(qh * kk, axis=-1, keepdims=True))
    for l in range(N_LEVELS):
        m = 1 << l
        zs = []
        for i in range(N_SLABS):
            base = i * SUBLANES
            if m >= SUBLANES:
                bnd = row(base - base % (2 * m) + m - 1)
                if base % (2 * m) >= m:
                    zs.append(qs[i] * jnp.exp2(bs[i] - bnd))
                else:
                    zs.append(ks[i] * jnp.exp2(bnd - bs[i]))
            elif l == 0:
                zs.append(jnp.where(right_lo[0], qs[i] * fs[i], ks[i]))
            else:
                bnd = row(base + m - 1)
                for pp in range(1, SUBLANES // (2 * m)):
                    bnd = jnp.where(sub >= pp * 2 * m, row(base + pp * 2 * m + m - 1), bnd)
                e = (bs[i] - bnd) * sgn_lo[l]
                zs.append(jnp.where(right_lo[l], qs[i], ks[i]) * jnp.exp2(e))
        z = jnp.concatenate(zs, axis=0)
        s_l = jnp.dot(z.astype(BF16), z.T.astype(BF16), preferred_element_type=F32)
        for i in range(N_SLABS):
            base = i * SUBLANES
            if m >= SUBLANES and base % (2 * m) < m:
                continue
            sc[i] = sc[i] + msk_ref[l, pl.ds(base, SUBLANES), :] * s_l[base:base + SUBLANES]
    scores = jnp.concatenate(sc, axis=0)
    o_intra = jnp.dot(scores.astype(BF16), v, preferred_element_type=F32)

    st = st_ref[...]
    q_in = (qh * jnp.exp2(b)).astype(BF16)
    o_inter = lax.dot_general(q_in, st.astype(BF16), _NT, preferred_element_type=F32)
    b_last = b_ref[pl.ds(b_row0 + CHUNK - 1, 1), cols]
    k_up = (kk * jnp.exp2(b_last - b)).astype(BF16)
    st_ref[...] = st * jnp.exp2(b_last) + lax.dot_general(v, k_up, _TN, preferred_element_type=F32)

    o = o_intra + o_inter
    o = o * lax.rsqrt(jnp.mean(o * o, axis=-1, keepdims=True) + RMS_EPS) * gn
    return (o * (ogr * _sigmoid(ogr))).astype(BF16)


def _conv_rows_group(ubuf_ref, r0, cols, w_ref, cb, ng, nb):
    base = r0 + (CONV_HALO - (CONV_KERNEL - 1))
    acc = jnp.broadcast_to(cb, (CONV_ROWS, LANES))
    for r in range(SUBLANES):
        taps = list(range(r, CONV_KERNEL, SUBLANES))
        xr = ubuf_ref[pl.ds(base + r, CONV_ROWS + (len(taps) - 1) * SUBLANES), :]
        for a, k in enumerate(taps):
            acc = acc + w_ref[pl.ds(k, 1), cols] * xr[a * SUBLANES:a * SUBLANES + CONV_ROWS, :]
    y = _layer_norm(acc, ng, nb, LN_EPS)
    return (y * _sigmoid(y)).astype(BF16)


def _mixer_kernel(n_tiles, tiles_per_seq,
                  a_ref, gate_ref, q_ref, f_ref, i_ref, og_ref, cw_ref, ccb_ref, cng_ref, cnb_ref,
                  lbl_ref, gn_ref, msk_ref, tri_ref, eye_ref, wu_ref, wd_ref,
                  x_ref, g0_ref, b0_ref, wout_ref, g1_ref, b1_ref,
                  h_ref, hb_ref, wub_ref, wdb_ref,
                  st_ref, b_ref, fg_ref, ubuf_ref, uo_ref, mix_ref):
    tt = TT_HGRN
    s = pl.program_id(0)
    seq_start = (jnp.minimum(s, n_tiles - 1) % tiles_per_seq) == 0
    slot_w = s % 2
    slot_r = 1 - slot_w

    @pl.when(s == 0)
    def _():
        uo_ref[...] = jnp.zeros(uo_ref.shape, BF16)

    @pl.when(seq_start)
    def _():
        st_ref[...] = jnp.zeros((HGRN_HEADS, HEAD_DIM, HEAD_DIM), F32)
        ubuf_ref[:, pl.ds(0, CONV_HALO), :] = jnp.zeros((CONV_GROUPS, CONV_HALO, LANES), F32)

    @pl.when(jnp.logical_not(seq_start))
    def _():
        ubuf_ref[:, pl.ds(0, CONV_HALO), :] = ubuf_ref[:, pl.ds(tt, CONV_HALO), :]

    for f in range(D_FF // TF_FFN):
        wub_ref[:, pl.ds(2 * f * TF_FFN, TF_FFN)] = wu_ref[:, pl.ds(f * TF_FFN, TF_FFN)].astype(BF16)
        wub_ref[:, pl.ds((2 * f + 1) * TF_FFN, TF_FFN)] = (
            wu_ref[:, pl.ds(D_FF + f * TF_FFN, TF_FFN)].astype(BF16))
    wdb_ref[...] = wd_ref[...].astype(BF16)

    rows = [lbl_ref[pl.ds(r, 1), :] for r in range(DEPTH + 1)]
    mx = functools.reduce(jnp.maximum, rows)
    ex = [jnp.exp(r - mx) for r in rows]
    lb_all = ex[0] / functools.reduce(lambda a, c: a + c, ex)
    gn_all = gn_ref[...]

    tri = tri_ref[...]
    for rb in range(tt // CUM_ROWS):
        crows = pl.ds(rb * CUM_ROWS, CUM_ROWS)
        fg = lb_all + (1.0 - lb_all) * _sigmoid(f_ref[crows, :].astype(F32))
        g2 = jnp.log(fg) * LOG2E
        g_hi = g2.astype(BF16)
        g_lo = (g2 - g_hi.astype(F32)).astype(BF16)
        b_ref[crows, :] = (jnp.dot(tri, g_hi, preferred_element_type=F32)
                           + jnp.dot(tri, g_lo, preferred_element_type=F32))
        fg_ref[crows, :] = fg

    sub = lax.broadcasted_iota(jnp.int32, (SUBLANES, LANES), 0)
    right_lo = [(sub % (2 << l)) >= (1 << l) for l in range(3)]
    sgn_lo = [jnp.where(r, 1.0, -1.0).astype(F32) for r in right_lo]
    eye = eye_ref[...]
    ccb_all = ccb_ref[...]
    cng_all = cng_ref[...]
    cnb_all = cnb_ref[...]

    n_units = (tt // CHUNK) * HGRN_HEADS
    out_cols = D_MODEL // OUT_PIECES
    n_ln = tt // LN_ROWS_INLINE
    ln_per_unit = -(-n_ln // (n_units - OUT_PIECES))

    for c in range(tt // CHUNK):
        r0 = c * CHUNK
        rws = pl.ds(r0, CHUNK)
        for h in range(HGRN_HEADS):
            cols = pl.ds(h * HEAD_DIM, HEAD_DIM)
            lane = slice(h * HEAD_DIM, (h + 1) * HEAD_DIM)
            ubuf_ref[h, pl.ds(CONV_HALO + r0, CHUNK), :] = (
                a_ref[rws, cols].astype(F32) * _sigmoid(gate_ref[rws, cols].astype(F32)))
            uo_ref[slot_w, rws, pl.ds(CONV_WIDTH + h * HEAD_DIM, HEAD_DIM)] = _hgrn_chunk_head(
                q_ref[rws, cols].astype(F32), i_ref[rws, cols], og_ref[rws, cols].astype(F32),
                fg_ref[rws, cols], b_ref[rws, cols], b_ref, r0, cols, gn_all[:, lane], st_ref.at[h],
                msk_ref, eye, sub, right_lo, sgn_lo)
            for j in range(CHUNK // CONV_ROWS):
                rr = r0 + j * CONV_ROWS
                uo_ref[slot_w, pl.ds(rr, CONV_ROWS), cols] = _conv_rows_group(
                    ubuf_ref.at[h], rr, cols, cw_ref, ccb_all[:, lane], cng_all[:, lane],
                    cnb_all[:, lane])

            unit = c * HGRN_HEADS + h
            if unit < OUT_PIECES:
                pc = pl.ds(unit * out_cols, out_cols)
                mix_ref[:, pc] = jnp.dot(uo_ref[slot_r], wout_ref[:, pc], preferred_element_type=F32)
            else:
                for r in range((unit - OUT_PIECES) * ln_per_unit,
                               min((unit - OUT_PIECES + 1) * ln_per_unit, n_ln)):
                    rows = pl.ds(r * LN_ROWS_INLINE, LN_ROWS_INLINE)
                    h0 = _layer_norm(x_ref[rows, :], g0_ref[...], b0_ref[...], LN_EPS)
                    y = ALPHA * h0 + mix_ref[rows, :]
                    h1 = _layer_norm(y, g1_ref[...], b1_ref[...], LN_EPS)
                    h_ref[rows, :] = h1
                    hb_ref[rows, :] = h1.astype(BF16)


def _mixer(proj, conv_w, conv_b, conv_ng, conv_nb, lb_logits, gn, w_up, w_down,
           x2, g0, b0, w_out_bf16, g1, b1, batch, seq):
    assert CONV_GROUPS == HGRN_HEADS and CONV_WIDTH == HGRN_WIDTH
    m = proj.shape[0]
    nt = seq // TT_HGRN
    n_steps = batch * nt
    msk, tri, eye = _hgrn_constants()
    col0 = 2 * CONV_WIDTH // HGRN_WIDTH
    up_rows = w_up.shape[0] // n_steps
    down_rows = w_down.shape[0] // n_steps
    assert up_rows * n_steps == w_up.shape[0] and down_rows * n_steps == w_down.shape[0]
    assert up_rows % 16 == 0 and down_rows % 16 == 0
    cur = lambda s: jnp.minimum(s, n_steps - 1)
    prev = lambda s: jnp.maximum(s - 1, 0)

    def col(n):
        return lambda s: (cur(s), n)

    slab = lambda s: (cur(s), 0)
    prow = lambda s: (prev(s), 0)
    const2 = lambda s: (0, 0)
    return pl.pallas_call(
        functools.partial(_mixer_kernel, n_steps, nt),
        grid=(n_steps + 1,),
        in_specs=[
            pl.BlockSpec((TT_HGRN, CONV_WIDTH), col(0)),
            pl.BlockSpec((TT_HGRN, CONV_WIDTH), col(1)),
            pl.BlockSpec((TT_HGRN, HGRN_WIDTH), col(col0)),
            pl.BlockSpec((TT_HGRN, HGRN_WIDTH), col(col0 + 1)),
            pl.BlockSpec((TT_HGRN, HGRN_WIDTH), col(col0 + 2)),
            pl.BlockSpec((TT_HGRN, HGRN_WIDTH), col(col0 + 3)),
            pl.BlockSpec((CONV_KERNEL, CONV_WIDTH), const2),
            pl.BlockSpec((1, CONV_WIDTH), const2),
            pl.BlockSpec((1, CONV_WIDTH), const2),
            pl.BlockSpec((1, CONV_WIDTH), const2),
            pl.BlockSpec((DEPTH + 1, HGRN_WIDTH), const2),
            pl.BlockSpec((1, HGRN_WIDTH), const2),
            pl.BlockSpec((N_LEVELS, CHUNK, CHUNK), lambda s: (0, 0, 0)),
            pl.BlockSpec((CUM_ROWS, CUM_ROWS), const2),
            pl.BlockSpec((CHUNK, CHUNK), const2),
            pl.BlockSpec((up_rows, w_up.shape[1]), slab),
            pl.BlockSpec((down_rows, w_down.shape[1]), slab),
            pl.BlockSpec((TT_HGRN, D_MODEL), prow),
            pl.BlockSpec((1, D_MODEL), const2),
            pl.BlockSpec((1, D_MODEL), const2),
            pl.BlockSpec((D_MODEL, D_MODEL), const2, pipeline_mode=pl.Buffered(1)),
            pl.BlockSpec((1, D_MODEL), const2),
            pl.BlockSpec((1, D_MODEL), const2),
        ],
        out_specs=[
            pl.BlockSpec((TT_HGRN, D_MODEL), prow),
            pl.BlockSpec((TT_HGRN, D_MODEL), prow),
            pl.BlockSpec((up_rows, w_up.shape[1]), slab),
            pl.BlockSpec((down_rows, w_down.shape[1]), slab),
        ],
        out_shape=[
            jax.ShapeDtypeStruct((m, D_MODEL), F32),
            jax.ShapeDtypeStruct((m, D_MODEL), BF16),
            jax.ShapeDtypeStruct(w_up.shape, BF16),
            jax.ShapeDtypeStruct(w_down.shape, BF16),
        ],
        scratch_shapes=[pltpu.VMEM((HGRN_HEADS, HEAD_DIM, HEAD_DIM), F32),
                        pltpu.VMEM((TT_HGRN, HGRN_WIDTH), F32),
                        pltpu.VMEM((TT_HGRN, HGRN_WIDTH), F32),
                        pltpu.VMEM((CONV_GROUPS, CONV_HALO + TT_HGRN, LANES), F32),
                        pltpu.VMEM((2, TT_HGRN, D_MODEL), BF16),
                        pltpu.VMEM((TT_HGRN, D_MODEL), F32)],
        compiler_params=pltpu.CompilerParams(
            dimension_semantics=("arbitrary",),
            vmem_limit_bytes=VMEM_LIMIT),
        name="mixer",
    )(proj, proj, proj, proj, proj, proj, conv_w, conv_b, conv_ng, conv_nb, lb_logits.astype(F32), gn,
      jnp.asarray(msk), jnp.asarray(tri, dtype=BF16), jnp.asarray(eye), w_up, w_down,
      x2, g0, b0, w_out_bf16, g1, b1)


def _ffn_kernel(tiles_per_seq, nf, hb_ref, hres_ref, wgv_ref, cw_ref, cb_ref, wd_ref, g2_ref,
                b2_ref, o_ref, acc_ref, g_ref, v_ref, tail_ref):
    s = pl.program_id(0)
    tm = TM_FFN
    n_slab = TF_FFN // LANES
    sb = jnp.maximum(s - 1, 0)
    i_b = sb // nf
    f_b = sb % nf

    @pl.when(s == 0)
    def _():
        acc_ref[...] = jnp.zeros(acc_ref.shape, F32)
        g_ref[...] = jnp.zeros(g_ref.shape, F32)
        v_ref[...] = jnp.zeros(v_ref.shape, F32)
        tail_ref[...] = jnp.zeros(tail_ref.shape, F32)

    def step(slot_a, slot_b):
        gv = jnp.dot(hb_ref[...], wgv_ref[...], preferred_element_type=F32)
        for j in range(n_slab):
            g_ref[slot_a, j, pl.ds(SUBLANES, tm), :] = gv[:, j * LANES:(j + 1) * LANES]
        v_ref[slot_a] = gv[:, TF_FFN:]

        seq_start = (i_b % tiles_per_seq) == 0
        g_ref[slot_b, :, pl.ds(0, SUBLANES), :] = jnp.where(seq_start, 0.0, tail_ref[f_b])
        tail_ref[f_b] = g_ref[slot_b, :, pl.ds(tm, SUBLANES), :]
        part = None
        slabs_per_k = FFN_K_COLS // LANES
        for kc in range(TF_FFN // FFN_K_COLS):
            acts = []
            for j in range(kc * slabs_per_k, (kc + 1) * slabs_per_k):
                cols = pl.ds(j * LANES, LANES)
                conv = (cb_ref[:, cols]
                        + cw_ref[pl.ds(2, 1), cols] * g_ref[slot_b, j, pl.ds(SUBLANES, tm), :]
                        + cw_ref[pl.ds(1, 1), cols] * g_ref[slot_b, j, pl.ds(SUBLANES - 1, tm), :]
                        + cw_ref[pl.ds(0, 1), cols] * g_ref[slot_b, j, pl.ds(SUBLANES - 2, tm), :])
                acts.append((conv * _sigmoid(conv) * v_ref[slot_b, :, cols]).astype(BF16))
            act = jnp.concatenate(acts, axis=1)
            d = jnp.dot(act, wd_ref[pl.ds(kc * FFN_K_COLS, FFN_K_COLS), :], preferred_element_type=F32)
            part = d if part is None else part + d
        acc_ref[...] += part

    @pl.when(s % 2 == 0)
    def _():
        step(0, 1)

    @pl.when(s % 2 == 1)
    def _():
        step(1, 0)

    @pl.when(jnp.logical_and(s > 0, f_b == nf - 1))
    def _():
        g2 = g2_ref[...]
        b2 = b2_ref[...]
        for r in range(tm // LN_ROWS_INLINE):
            rows = pl.ds(r * LN_ROWS_INLINE, LN_ROWS_INLINE)
            y = ALPHA * hres_ref[rows, :] + acc_ref[rows, :]
            o_ref[rows, :] = _layer_norm(y, g2, b2, LN_EPS)
            acc_ref[rows, :] = jnp.zeros((LN_ROWS_INLINE, D_MODEL), F32)


def _ffn(h1, h1_bf16, w_up_bf16, cw, cb, w_down_bf16, g2, b2, seq):
    m = h1.shape[0]
    nf = D_FF // TF_FFN
    n_steps = (m // TM_FFN) * nf
    a_step = lambda s: jnp.minimum(s, n_steps - 1)
    b_step = lambda s: jnp.maximum(s - 1, 0)
    return pl.pallas_call(
        functools.partial(_ffn_kernel, seq // TM_FFN, nf),
        grid=(n_steps + 1,),
        in_specs=[
            pl.BlockSpec((TM_FFN, D_MODEL), lambda s: (a_step(s) // nf, 0)),
            pl.BlockSpec((TM_FFN, D_MODEL), lambda s: (b_step(s) // nf, 0)),
            pl.BlockSpec((D_MODEL, 2 * TF_FFN), lambda s: (0, a_step(s) % nf)),
            pl.BlockSpec((FFN_KERNEL, TF_FFN), lambda s: (0, b_step(s) % nf)),
            pl.BlockSpec((1, TF_FFN), lambda s: (0, b_step(s) % nf)),
            pl.BlockSpec((TF_FFN, D_MODEL), lambda s: (b_step(s) % nf, 0)),
            pl.BlockSpec((1, D_MODEL), lambda s: (0, 0)),
            pl.BlockSpec((1, D_MODEL), lambda s: (0, 0)),
        ],
        out_specs=pl.BlockSpec((TM_FFN, D_MODEL), lambda s: (b_step(s) // nf, 0)),
        out_shape=jax.ShapeDtypeStruct((m, D_MODEL), F32),
        scratch_shapes=[
            pltpu.VMEM((TM_FFN, D_MODEL), F32),
            pltpu.VMEM((2, TF_FFN // LANES, SUBLANES + TM_FFN, LANES), F32),
            pltpu.VMEM((2, TM_FFN, TF_FFN), F32),
            pltpu.VMEM((nf, TF_FFN // LANES, SUBLANES, LANES), F32),
        ],
        compiler_params=pltpu.CompilerParams(
            dimension_semantics=("arbitrary",),
            vmem_limit_bytes=VMEM_LIMIT),
        name="conv_ffn_ln2",
    )(h1_bf16, h1, w_up_bf16, cw, cb, w_down_bf16, g2, b2)


def kernel(x, emb_ln_g, emb_ln_b, w_in, conv_w, conv_b, conv_norm_g, conv_norm_b, lb_logits,
           hgrn_norm_g, w_out, ln1_g, ln1_b, w_ffn_up, ffn_conv_w, ffn_conv_b, w_ffn_down,
           ln2_g, ln2_b):
    batch, seq, d = x.shape
    assert d == D_MODEL and w_in.shape[0] == DEPTH == 1
    assert seq % TT_HGRN == 0 and seq % TM_FFN == 0
    x2 = x.reshape(batch * seq, d)
    row = lambda a: a.reshape(1, -1).astype(F32)

    proj = _inproj(x2, row(emb_ln_g), row(emb_ln_b), w_in[0])
    h1, h1_bf16, w_up_bf16, w_down_bf16 = _mixer(
        proj, conv_w[0], row(conv_b[0]), row(conv_norm_g[0]), row(conv_norm_b[0]), lb_logits,
        row(hgrn_norm_g[0]), w_ffn_up[0], w_ffn_down[0],
        x2, row(emb_ln_g), row(emb_ln_b), w_out[0].astype(BF16), row(ln1_g[0]), row(ln1_b[0]),
        batch, seq)
    out = _ffn(h1, h1_bf16, w_up_bf16, ffn_conv_w[0], row(ffn_conv_b[0]),
               w_down_bf16, row(ln2_g[0]), row(ln2_b[0]), seq)
    return out.reshape(batch, seq, d)
```

```python
import functools

import jax
import jax.numpy as jnp
import numpy as np
from jax import lax
from jax.experimental import pallas as pl
from jax.experimental.pallas import tpu as pltpu

F32 = jnp.float32
BF16 = jnp.bfloat16

D_MODEL = 2048
CONV_WIDTH = 1024
CONV_GROUPS = 8
CONV_KERNEL = 31
HGRN_WIDTH = 1024
HGRN_HEADS = 8
HEAD_DIM = 128
IN_PROJ_DIM = 2 * CONV_WIDTH + 4 * HGRN_WIDTH
D_FF = 5632
FFN_KERNEL = 3
LN_EPS = 1e-5
RMS_EPS = 1e-6
DEPTH = 1
ALPHA = (2.0 * DEPTH) ** 0.25
LOG2E = 1.4426950408889634

LANES = 128
SUBLANES = 8
VMEM_LIMIT = 56 * 1024 * 1024

TM_IN, TN_IN = 1024, 512
IN_PIECE_ROWS = 256
CONV_HALO = 32
TT_HGRN = 256
OUT_PIECES = 8
CHUNK = 128
N_LEVELS = 7
CUM_ROWS = 256
CONV_ROWS = 64
N_SLABS = CHUNK // SUBLANES
TM_FFN, TF_FFN = 512, 512
FFN_K_COLS = 256
LN_ROWS = 64
LN_ROWS_INLINE = 16

_NT = (((1,), (1,)), ((), ()))
_TN = (((0,), (0,)), ((), ()))


def _layer_norm(x, g, b, eps):
    mu = jnp.mean(x, axis=-1, keepdims=True)
    xc = x - mu
    var = jnp.mean(xc * xc, axis=-1, keepdims=True)
    return xc * lax.rsqrt(var + eps) * g + b


def _sigmoid(x):
    return 1.0 / (1.0 + jnp.exp2(x * (-LOG2E)))


def _inproj_kernel(x_ref, g_ref, b_ref, w_ref, o_ref, h0_ref, xn_ref):
    @pl.when(pl.program_id(1) == 0)
    def _():
        g = g_ref[...]
        b = b_ref[...]
        wb = w_ref[...].astype(BF16)
        for p in range(TM_IN // IN_PIECE_ROWS):
            for r in range(IN_PIECE_ROWS // LN_ROWS_INLINE):
                rows = pl.ds(p * IN_PIECE_ROWS + r * LN_ROWS_INLINE, LN_ROWS_INLINE)
                h0 = _layer_norm(x_ref[rows, :], g, b, LN_EPS)
                h0_ref[rows, :] = h0
                xn_ref[rows, :] = h0.astype(BF16)
            prows = pl.ds(p * IN_PIECE_ROWS, IN_PIECE_ROWS)
            o_ref[prows, :] = jnp.dot(xn_ref[prows, :], wb, preferred_element_type=F32).astype(BF16)

    @pl.when(pl.program_id(1) != 0)
    def _():
        o_ref[...] = jnp.dot(xn_ref[...], w_ref[...].astype(BF16),
                             preferred_element_type=F32).astype(BF16)


def _inproj(x2, g, b, w_f32):
    m = x2.shape[0]
    return pl.pallas_call(
        _inproj_kernel,
        grid=(m // TM_IN, IN_PROJ_DIM // TN_IN),
        in_specs=[
            pl.BlockSpec((TM_IN, D_MODEL), lambda i, j: (i, 0)),
            pl.BlockSpec((1, D_MODEL), lambda i, j: (0, 0)),
            pl.BlockSpec((1, D_MODEL), lambda i, j: (0, 0)),
            pl.BlockSpec((D_MODEL, TN_IN), lambda i, j: (0, j)),
        ],
        out_specs=[pl.BlockSpec((TM_IN, TN_IN), lambda i, j: (i, j)),
                   pl.BlockSpec((TM_IN, D_MODEL), lambda i, j: (i, 0))],
        out_shape=[jax.ShapeDtypeStruct((m, IN_PROJ_DIM), BF16),
                   jax.ShapeDtypeStruct((m, D_MODEL), F32)],
        scratch_shapes=[pltpu.VMEM((TM_IN, D_MODEL), BF16)],
        compiler_params=pltpu.CompilerParams(
            dimension_semantics=("arbitrary", "arbitrary"),
            vmem_limit_bytes=VMEM_LIMIT),
        name="ln_inproj",
    )(x2, g, b, w_f32)


def _hgrn_constants():
    t = np.arange(CHUNK)
    msk = np.zeros((N_LEVELS, CHUNK, CHUNK), np.float32)
    for l in range(N_LEVELS):
        m = 1 << l
        right = (t % (2 * m)) >= m
        same = (t[:, None] // (2 * m)) == (t[None, :] // (2 * m))
        msk[l] = (same & right[:, None] & (~right)[None, :]).astype(np.float32)
    r = np.arange(CUM_ROWS)
    tri = ((r[:, None] >= r[None, :]) & (r[:, None] // CHUNK == r[None, :] // CHUNK)).astype(np.float32)
    eye = np.eye(CHUNK, dtype=np.float32)
    return msk, tri, eye


def _hgrn_chunk_head(qr, vr, ogr, fg, b, b_ref, b_row0, cols, gn, st_ref, msk_ref, eye, sub, right_lo,
                     sgn_lo):
    kk = 1.0 - fg
    qh = qr * _sigmoid(qr)
    v = vr

    def slabs(x):
        return [x[i * SUBLANES:(i + 1) * SUBLANES] for i in range(N_SLABS)]

    def row(r):
        return jnp.broadcast_to(b_ref[pl.ds(b_row0 + r, 1), cols], (SUBLANES, LANES))

    bs, qs, ks, fs = slabs(b), slabs(qh), slabs(kk), slabs(fg)
    sc = slabs(eye * jnp.sum(qh * kk, axis=-1, keepdims=True))
    for l in range(N_LEVELS):
        m = 1 << l
        zs = []
        for i in range(N_SLABS):
            base = i * SUBLANES
            if m >= SUBLANES:
                bnd = row(base - base % (2 * m) + m - 1)
                if base % (2 * m) >= m:
                    zs.append(qs[i] * jnp.exp2(bs[i] - bnd))
                else:
                    zs.append(ks[i] * jnp.exp2(bnd - bs[i]))
            elif l == 0:
                zs.append(jnp.where(right_lo[0], qs[i] * fs[i], ks[i]))
            else:
                bnd = row(base + m - 1)
                for pp in range(1, SUBLANES // (2 * m)):
                    bnd = jnp.where(sub >= pp * 2 * m, row(base + pp * 2 * m + m - 1), bnd)
                e = (bs[i] - bnd) * sgn_lo[l]
                zs.append(jnp.where(right_lo[l], qs[i], ks[i]) * jnp.exp2(e))
        z = jnp.concatenate(zs, axis=0)
        s_l = jnp.dot(z.astype(BF16), z.T.astype(BF16), preferred_element_type=F32)
        for i in range(N_SLABS):
            base = i * SUBLANES
            if m >= SUBLANES and base % (2 * m) < m:
                continue
            sc[i] = sc[i] + msk_ref[l, pl.ds(base, SUBLANES), :] * s_l[base:base + SUBLANES]
    scores = jnp.concatenate(sc, axis=0)
    o_intra = jnp.dot(scores.astype(BF16), v, preferred_element_type=F32)

    st = st_ref[...]
    q_in = (qh * jnp.exp2(b)).astype(BF16)
    o_inter = lax.dot_general(q_in, st.astype(BF16), _NT, preferred_element_type=F32)
    b_last = b_ref[pl.ds(b_row0 + CHUNK - 1, 1), cols]
    k_up = (kk * jnp.exp2(b_last - b)).astype(BF16)
    st_ref[...] = st * jnp.exp2(b_last) + lax.dot_general(v, k_up, _TN, preferred_element_type=F32)

    o = o_intra + o_inter
    o = o * lax.rsqrt(jnp.mean(o * o, axis=-1, keepdims=True) + RMS_EPS) * gn
    return (o * (ogr * _sigmoid(ogr))).astype(BF16)


def _conv_rows_group(ubuf_ref, r0, cols, w_ref, cb, ng, nb):
    base = r0 + (CONV_HALO - (CONV_KERNEL - 1))
    acc = jnp.broadcast_to(cb, (CONV_ROWS, LANES))
    for r in range(SUBLANES):
        taps = list(range(r, CONV_KERNEL, SUBLANES))
        xr = ubuf_ref[pl.ds(base + r, CONV_ROWS + (len(taps) - 1) * SUBLANES), :]
        for a, k in enumerate(taps):
            acc = acc + w_ref[pl.ds(k, 1), cols] * xr[a * SUBLANES:a * SUBLANES + CONV_ROWS, :]
    y = _layer_norm(acc, ng, nb, LN_EPS)
    return (y * _sigmoid(y)).astype(BF16)


def _mixer_kernel(n_tiles, tiles_per_seq,
                  a_ref, gate_ref, q_ref, f_ref, i_ref, og_ref, cw_ref, ccb_ref, cng_ref, cnb_ref,
                  lbl_ref, gn_ref, msk_ref, tri_ref, eye_ref, wu_ref, wd_ref,
                  h0_ref, wout_ref, g1_ref, b1_ref,
                  h_ref, hb_ref, wub_ref, wdb_ref,
                  st_ref, b_ref, fg_ref, ubuf_ref, uo_ref, mix_ref):
    tt = TT_HGRN
    s = pl.program_id(0)
    seq_start = (jnp.minimum(s, n_tiles - 1) % tiles_per_seq) == 0
    slot_w = s % 2
    slot_r = 1 - slot_w

    @pl.when(s == 0)
    def _():
        uo_ref[...] = jnp.zeros(uo_ref.shape, BF16)

    @pl.when(seq_start)
    def _():
        st_ref[...] = jnp.zeros((HGRN_HEADS, HEAD_DIM, HEAD_DIM), F32)
        ubuf_ref[:, pl.ds(0, CONV_HALO), :] = jnp.zeros((CONV_GROUPS, CONV_HALO, LANES), F32)

    @pl.when(jnp.logical_not(seq_start))
    def _():
        ubuf_ref[:, pl.ds(0, CONV_HALO), :] = ubuf_ref[:, pl.ds(tt, CONV_HALO), :]

    for f in range(D_FF // TF_FFN):
        wub_ref[:, pl.ds(2 * f * TF_FFN, TF_FFN)] = wu_ref[:, pl.ds(f * TF_FFN, TF_FFN)].astype(BF16)
        wub_ref[:, pl.ds((2 * f + 1) * TF_FFN, TF_FFN)] = (
            wu_ref[:, pl.ds(D_FF + f * TF_FFN, TF_FFN)].astype(BF16))
    wdb_ref[...] = wd_ref[...].astype(BF16)

    rows = [lbl_ref[pl.ds(r, 1), :] for r in range(DEPTH + 1)]
    mx = functools.reduce(jnp.maximum, rows)
    ex = [jnp.exp(r - mx) for r in rows]
    lb_all = ex[0] / functools.reduce(lambda a, c: a + c, ex)
    gn_all = gn_ref[...]

    tri = tri_ref[...]
    for rb in range(tt // CUM_ROWS):
        crows = pl.ds(rb * CUM_ROWS, CUM_ROWS)
        fg = lb_all + (1.0 - lb_all) * _sigmoid(f_ref[crows, :].astype(F32))
        g2 = jnp.log(fg) * LOG2E
        g_hi = g2.astype(BF16)
        g_lo = (g2 - g_hi.astype(F32)).astype(BF16)
        b_ref[crows, :] = (jnp.dot(tri, g_hi, preferred_element_type=F32)
                           + jnp.dot(tri, g_lo, preferred_element_type=F32))
        fg_ref[crows, :] = fg

    sub = lax.broadcasted_iota(jnp.int32, (SUBLANES, LANES), 0)
    right_lo = [(sub % (2 << l)) >= (1 << l) for l in range(3)]
    sgn_lo = [jnp.where(r, 1.0, -1.0).astype(F32) for r in right_lo]
    eye = eye_ref[...]
    ccb_all = ccb_ref[...]
    cng_all = cng_ref[...]
    cnb_all = cnb_ref[...]

    n_units = (tt // CHUNK) * HGRN_HEADS
    out_cols = D_MODEL // OUT_PIECES
    n_ln = tt // LN_ROWS_INLINE
    ln_per_unit = -(-n_ln // (n_units - OUT_PIECES))

    for c in range(tt // CHUNK):
        r0 = c * CHUNK
        rws = pl.ds(r0, CHUNK)
        for h in range(HGRN_HEADS):
            cols = pl.ds(h * HEAD_DIM, HEAD_DIM)
            lane = slice(h * HEAD_DIM, (h + 1) * HEAD_DIM)
            ubuf_ref[h, pl.ds(CONV_HALO + r0, CHUNK), :] = (
                a_ref[rws, cols].astype(F32) * _sigmoid(gate_ref[rws, cols].astype(F32)))
            uo_ref[slot_w, rws, pl.ds(CONV_WIDTH + h * HEAD_DIM, HEAD_DIM)] = _hgrn_chunk_head(
                q_ref[rws, cols].astype(F32), i_ref[rws, cols], og_ref[rws, cols].astype(F32),
                fg_ref[rws, cols], b_ref[rws, cols], b_ref, r0, cols, gn_all[:, lane], st_ref.at[h],
                msk_ref, eye, sub, right_lo, sgn_lo)
            for j in range(CHUNK // CONV_ROWS):
                rr = r0 + j * CONV_ROWS
                uo_ref[slot_w, pl.ds(rr, CONV_ROWS), cols] = _conv_rows_group(
                    ubuf_ref.at[h], rr, cols, cw_ref, ccb_all[:, lane], cng_all[:, lane],
                    cnb_all[:, lane])

            unit = c * HGRN_HEADS + h
            if unit < OUT_PIECES:
                pc = pl.ds(unit * out_cols, out_cols)
                mix_ref[:, pc] = jnp.dot(uo_ref[slot_r], wout_ref[:, pc], preferred_element_type=F32)
            else:
                for r in range((unit - OUT_PIECES) * ln_per_unit,
                               min((unit - OUT_PIECES + 1) * ln_per_unit, n_ln)):
                    rows = pl.ds(r * LN_ROWS_INLINE, LN_ROWS_INLINE)
                    y = ALPHA * h0_ref[rows, :] + mix_ref[rows, :]
                    h1 = _layer_norm(y, g1_ref[...], b1_ref[...], LN_EPS)
                    h_ref[rows, :] = h1
                    hb_ref[rows, :] = h1.astype(BF16)


def _mixer(proj, conv_w, conv_b, conv_ng, conv_nb, lb_logits, gn, w_up, w_down,
           h0, w_out_bf16, g1, b1, batch, seq):
    assert CONV_GROUPS == HGRN_HEADS and CONV_WIDTH == HGRN_WIDTH
    m = proj.shape[0]
    nt = seq // TT_HGRN
    n_steps = batch * nt
    msk, tri, eye = _hgrn_constants()
    col0 = 2 * CONV_WIDTH // HGRN_WIDTH
    up_rows = w_up.shape[0] // n_steps
    down_rows = w_down.shape[0] // n_steps
    assert up_rows * n_steps == w_up.shape[0] and down_rows * n_steps == w_down.shape[0]
    assert up_rows % 16 == 0 and down_rows % 16 == 0
    cur = lambda s: jnp.minimum(s, n_steps - 1)
    prev = lambda s: jnp.maximum(s - 1, 0)

    def col(n):
        return lambda s: (cur(s), n)

    slab = lambda s: (cur(s), 0)
    prow = lambda s: (prev(s), 0)
    const2 = lambda s: (0, 0)
    return pl.pallas_call(
        functools.partial(_mixer_kernel, n_steps, nt),
        grid=(n_steps + 1,),
        in_specs=[
            pl.BlockSpec((TT_HGRN, CONV_WIDTH), col(0)),
            pl.BlockSpec((TT_HGRN, CONV_WIDTH), col(1)),
            pl.BlockSpec((TT_HGRN, HGRN_WIDTH), col(col0)),
            pl.BlockSpec((TT_HGRN, HGRN_WIDTH), col(col0 + 1)),
            pl.BlockSpec((TT_HGRN, HGRN_WIDTH), col(col0 + 2)),
            pl.BlockSpec((TT_HGRN, HGRN_WIDTH), col(col0 + 3)),
            pl.BlockSpec((CONV_KERNEL, CONV_WIDTH), const2),
            pl.BlockSpec((1, CONV_WIDTH), const2),
            pl.BlockSpec((1, CONV_WIDTH), const2),
            pl.BlockSpec((1, CONV_WIDTH), const2),
            pl.BlockSpec((DEPTH + 1, HGRN_WIDTH), const2),
            pl.BlockSpec((1, HGRN_WIDTH), const2),
            pl.BlockSpec((N_LEVELS, CHUNK, CHUNK), lambda s: (0, 0, 0)),
            pl.BlockSpec((CUM_ROWS, CUM_ROWS), const2),
            pl.BlockSpec((CHUNK, CHUNK), const2),
            pl.BlockSpec((up_rows, w_up.shape[1]), slab),
            pl.BlockSpec((down_rows, w_down.shape[1]), slab),
            pl.BlockSpec((TT_HGRN, D_MODEL), prow),
            pl.BlockSpec((D_MODEL, D_MODEL), const2, pipeline_mode=pl.Buffered(1)),
            pl.BlockSpec((1, D_MODEL), const2),
            pl.BlockSpec((1, D_MODEL), const2),
        ],
        out_specs=[
            pl.BlockSpec((TT_HGRN, D_MODEL), prow),
            pl.BlockSpec((TT_HGRN, D_MODEL), prow),
            pl.BlockSpec((up_rows, w_up.shape[1]), slab),
            pl.BlockSpec((down_rows, w_down.shape[1]), slab),
        ],
        out_shape=[
            jax.ShapeDtypeStruct((m, D_MODEL), F32),
            jax.ShapeDtypeStruct((m, D_MODEL), BF16),
            jax.ShapeDtypeStruct(w_up.shape, BF16),
            jax.ShapeDtypeStruct(w_down.shape, BF16),
        ],
        scratch_shapes=[pltpu.VMEM((HGRN_HEADS, HEAD_DIM, HEAD_DIM), F32),
                        pltpu.VMEM((TT_HGRN, HGRN_WIDTH), F32),
                        pltpu.VMEM((TT_HGRN, HGRN_WIDTH), F32),
                        pltpu.VMEM((CONV_GROUPS, CONV_HALO + TT_HGRN, LANES), F32),
                        pltpu.VMEM((2, TT_HGRN, D_MODEL), BF16),
                        pltpu.VMEM((TT_HGRN, D_MODEL), F32)],
        compiler_params=pltpu.CompilerParams(
            dimension_semantics=("arbitrary",),
            vmem_limit_bytes=VMEM_LIMIT),
        name="mixer",
    )(proj, proj, proj, proj, proj, proj, conv_w, conv_b, conv_ng, conv_nb, lb_logits.astype(F32), gn,
      jnp.asarray(msk), jnp.asarray(tri, dtype=BF16), jnp.asarray(eye), w_up, w_down,
      h0, w_out_bf16, g1, b1)


def _ffn_kernel(tiles_per_seq, nf, hb_ref, hres_ref, wgv_ref, cw_ref, cb_ref, wd_ref, g2_ref,
                b2_ref, o_ref, acc_ref, g_ref, v_ref, tail_ref):
    s = pl.program_id(0)
    tm = TM_FFN
    n_slab = TF_FFN // LANES
    sb = jnp.maximum(s - 1, 0)
    i_b = sb // nf
    f_b = sb % nf

    @pl.when(s == 0)
    def _():
        acc_ref[...] = jnp.zeros(acc_ref.shape, F32)
        g_ref[...] = jnp.zeros(g_ref.shape, F32)
        v_ref[...] = jnp.zeros(v_ref.shape, F32)
        tail_ref[...] = jnp.zeros(tail_ref.shape, F32)

    def step(slot_a, slot_b):
        gv = jnp.dot(hb_ref[...], wgv_ref[...], preferred_element_type=F32)
        for j in range(n_slab):
            g_ref[slot_a, j, pl.ds(SUBLANES, tm), :] = gv[:, j * LANES:(j + 1) * LANES]
        v_ref[slot_a] = gv[:, TF_FFN:]

        seq_start = (i_b % tiles_per_seq) == 0
        g_ref[slot_b, :, pl.ds(0, SUBLANES), :] = jnp.where(seq_start, 0.0, tail_ref[f_b])
        tail_ref[f_b] = g_ref[slot_b, :, pl.ds(tm, SUBLANES), :]
        part = None
        slabs_per_k = FFN_K_COLS // LANES
        for kc in range(TF_FFN // FFN_K_COLS):
            acts = []
            for j in range(kc * slabs_per_k, (kc + 1) * slabs_per_k):
                cols = pl.ds(j * LANES, LANES)
                conv = (cb_ref[:, cols]
                        + cw_ref[pl.ds(2, 1), cols] * g_ref[slot_b, j, pl.ds(SUBLANES, tm), :]
                        + cw_ref[pl.ds(1, 1), cols] * g_ref[slot_b, j, pl.ds(SUBLANES - 1, tm), :]
                        + cw_ref[pl.ds(0, 1), cols] * g_ref[slot_b, j, pl.ds(SUBLANES - 2, tm), :])
                acts.append((conv * _sigmoid(conv) * v_ref[slot_b, :, cols]).astype(BF16))
            act = jnp.concatenate(acts, axis=1)
            d = jnp.dot(act, wd_ref[pl.ds(kc * FFN_K_COLS, FFN_K_COLS), :], preferred_element_type=F32)
            part = d if part is None else part + d
        acc_ref[...] += part

    @pl.when(s % 2 == 0)
    def _():
        step(0, 1)

    @pl.when(s % 2 == 1)
    def _():
        step(1, 0)

    @pl.when(jnp.logical_and(s > 0, f_b == nf - 1))
    def _():
        g2 = g2_ref[...]
        b2 = b2_ref[...]
        for r in range(tm // LN_ROWS_INLINE):
            rows = pl.ds(r * LN_ROWS_INLINE, LN_ROWS_INLINE)
            y = ALPHA * hres_ref[rows, :] + acc_ref[rows, :]
            o_ref[rows, :] = _layer_norm(y, g2, b2, LN_EPS)
            acc_ref[rows, :] = jnp.zeros((LN_ROWS_INLINE, D_MODEL), F32)


def _ffn(h1, h1_bf16, w_up_bf16, cw, cb, w_down_bf16, g2, b2, seq):
    m = h1.shape[0]
    nf = D_FF // TF_FFN
    n_steps = (m // TM_FFN) * nf
    a_step = lambda s: jnp.minimum(s, n_steps - 1)
    b_step = lambda s: jnp.maximum(s - 1, 0)
    return pl.pallas_call(
        functools.partial(_ffn_kernel, seq // TM_FFN, nf),
        grid=(n_steps + 1,),
        in_specs=[
            pl.BlockSpec((TM_FFN, D_MODEL), lambda s: (a_step(s) // nf, 0)),
            pl.BlockSpec((TM_FFN, D_MODEL), lambda s: (b_step(s) // nf, 0)),
            pl.BlockSpec((D_MODEL, 2 * TF_FFN), lambda s: (0, a_step(s) % nf)),
            pl.BlockSpec((FFN_KERNEL, TF_FFN), lambda s: (0, b_step(s) % nf)),
            pl.BlockSpec((1, TF_FFN), lambda s: (0, b_step(s) % nf)),
            pl.BlockSpec((TF_FFN, D_MODEL), lambda s: (b_step(s) % nf, 0)),
            pl.BlockSpec((1, D_MODEL), lambda s: (0, 0)),
            pl.BlockSpec((1, D_MODEL), lambda s: (0, 0)),
        ],
        out_specs=pl.BlockSpec((TM_FFN, D_MODEL), lambda s: (b_step(s) // nf, 0)),
        out_shape=jax.ShapeDtypeStruct((m, D_MODEL), F32),
        scratch_shapes=[
            pltpu.VMEM((TM_FFN, D_MODEL), F32),
            pltpu.VMEM((2, TF_FFN // LANES, SUBLANES + TM_FFN, LANES), F32),
            pltpu.VMEM((2, TM_FFN, TF_FFN), F32),
            pltpu.VMEM((nf, TF_FFN // LANES, SUBLANES, LANES), F32),
        ],
        compiler_params=pltpu.CompilerParams(
            dimension_semantics=("arbitrary",),
            vmem_limit_bytes=VMEM_LIMIT),
        name="conv_ffn_ln2",
    )(h1_bf16, h1, w_up_bf16, cw, cb, w_down_bf16, g2, b2)


def kernel(x, emb_ln_g, emb_ln_b, w_in, conv_w, conv_b, conv_norm_g, conv_norm_b, lb_logits,
           hgrn_norm_g, w_out, ln1_g, ln1_b, w_ffn_up, ffn_conv_w, ffn_conv_b, w_ffn_down,
           ln2_g, ln2_b):
    batch, seq, d = x.shape
    assert d == D_MODEL and w_in.shape[0] == DEPTH == 1
    assert seq % TT_HGRN == 0 and seq % TM_FFN == 0
    x2 = x.reshape(batch * seq, d)
    row = lambda a: a.reshape(1, -1).astype(F32)

    proj, h0 = _inproj(x2, row(emb_ln_g), row(emb_ln_b), w_in[0])
    h1, h1_bf16, w_up_bf16, w_down_bf16 = _mixer(
        proj, conv_w[0], row(conv_b[0]), row(conv_norm_g[0]), row(conv_norm_b[0]), lb_logits,
        row(hgrn_norm_g[0]), w_ffn_up[0], w_ffn_down[0],
        h0, w_out[0].astype(BF16), row(ln1_g[0]), row(ln1_b[0]), batch, seq)
    out = _ffn(h1, h1_bf16, w_up_bf16, ffn_conv_w[0], row(ffn_conv_b[0]),
               w_down_bf16, row(ln2_g[0]), row(ln2_b[0]), seq)
    return out.reshape(batch, seq, d)
```

```python
import functools

import jax
import jax.numpy as jnp
import numpy as np
from jax import lax
from jax.experimental import pallas as pl
from jax.experimental.pallas import tpu as pltpu

F32 = jnp.float32
BF16 = jnp.bfloat16

D_MODEL = 2048
CONV_WIDTH = 1024
CONV_GROUPS = 8
CONV_KERNEL = 31
HGRN_WIDTH = 1024
HGRN_HEADS = 8
HEAD_DIM = 128
IN_PROJ_DIM = 2 * CONV_WIDTH + 4 * HGRN_WIDTH
D_FF = 5632
FFN_KERNEL = 3
LN_EPS = 1e-5
RMS_EPS = 1e-6
DEPTH = 1
ALPHA = (2.0 * DEPTH) ** 0.25
LOG2E = 1.4426950408889634

LANES = 128
SUBLANES = 8
VMEM_LIMIT = 56 * 1024 * 1024

TM_IN, TN_IN = 1024, 1024
IN_PIECE_ROWS = 256
CONV_HALO = 32
TT_HGRN = 256
OUT_PIECES = 8
CHUNK = 128
N_LEVELS = 7
CUM_ROWS = 256
CONV_ROWS = 64
N_SLABS = CHUNK // SUBLANES
TM_FFN, TF_FFN = 512, 512
FFN_K_COLS = 256
LN_ROWS = 64
LN_ROWS_INLINE = 16

_NT = (((1,), (1,)), ((), ()))
_TN = (((0,), (0,)), ((), ()))


def _ln_stats(x, eps):
    mu = jnp.mean(x, axis=-1, keepdims=True)
    xc = x - mu
    var = jnp.mean(xc * xc, axis=-1, keepdims=True)
    return mu, lax.rsqrt(var + eps), xc


def _layer_norm(x, g, b, eps):
    _, rs, xc = _ln_stats(x, eps)
    return xc * rs * g + b


def _sigmoid(x):
    return 1.0 / (1.0 + jnp.exp2(x * (-LOG2E)))


def _inproj_kernel(x_ref, g_ref, b_ref, w_ref, o_ref, mu_ref, rs_ref, xn_ref):
    @pl.when(pl.program_id(1) == 0)
    def _():
        g = g_ref[...]
        b = b_ref[...]
        wb = w_ref[...].astype(BF16)
        for p in range(TM_IN // IN_PIECE_ROWS):
            for r in range(IN_PIECE_ROWS // LN_ROWS_INLINE):
                rows = pl.ds(p * IN_PIECE_ROWS + r * LN_ROWS_INLINE, LN_ROWS_INLINE)
                mu, rs, xc = _ln_stats(x_ref[rows, :], LN_EPS)
                mu_ref[rows, :] = jnp.broadcast_to(mu, (LN_ROWS_INLINE, LANES))
                rs_ref[rows, :] = jnp.broadcast_to(rs, (LN_ROWS_INLINE, LANES))
                xn_ref[rows, :] = (xc * rs * g + b).astype(BF16)
            prows = pl.ds(p * IN_PIECE_ROWS, IN_PIECE_ROWS)
            o_ref[prows, :] = jnp.dot(xn_ref[prows, :], wb, preferred_element_type=F32).astype(BF16)

    @pl.when(pl.program_id(1) != 0)
    def _():
        o_ref[...] = jnp.dot(xn_ref[...], w_ref[...].astype(BF16),
                             preferred_element_type=F32).astype(BF16)


def _inproj(x2, g, b, w_f32):
    m = x2.shape[0]
    return pl.pallas_call(
        _inproj_kernel,
        grid=(m // TM_IN, IN_PROJ_DIM // TN_IN),
        in_specs=[
            pl.BlockSpec((TM_IN, D_MODEL), lambda i, j: (i, 0)),
            pl.BlockSpec((1, D_MODEL), lambda i, j: (0, 0)),
            pl.BlockSpec((1, D_MODEL), lambda i, j: (0, 0)),
            pl.BlockSpec((D_MODEL, TN_IN), lambda i, j: (0, j)),
        ],
        out_specs=[pl.BlockSpec((TM_IN, TN_IN), lambda i, j: (i, j)),
                   pl.BlockSpec((TM_IN, LANES), lambda i, j: (i, 0)),
                   pl.BlockSpec((TM_IN, LANES), lambda i, j: (i, 0))],
        out_shape=[jax.ShapeDtypeStruct((m, IN_PROJ_DIM), BF16),
                   jax.ShapeDtypeStruct((m, LANES), F32),
                   jax.ShapeDtypeStruct((m, LANES), F32)],
        scratch_shapes=[pltpu.VMEM((TM_IN, D_MODEL), BF16)],
        compiler_params=pltpu.CompilerParams(
            dimension_semantics=("arbitrary", "arbitrary"),
            vmem_limit_bytes=VMEM_LIMIT),
        name="ln_inproj",
    )(x2, g, b, w_f32)


def _hgrn_constants():
    t = np.arange(CHUNK)
    msk = np.zeros((N_LEVELS, CHUNK, CHUNK), np.float32)
    for l in range(N_LEVELS):
        m = 1 << l
        right = (t % (2 * m)) >= m
        same = (t[:, None] // (2 * m)) == (t[None, :] // (2 * m))
        msk[l] = (same & right[:, None] & (~right)[None, :]).astype(np.float32)
    r = np.arange(CUM_ROWS)
    tri = ((r[:, None] >= r[None, :]) & (r[:, None] // CHUNK == r[None, :] // CHUNK)).astype(np.float32)
    eye = np.eye(CHUNK, dtype=np.float32)
    return msk, tri, eye


def _hgrn_chunk_head(qr, vr, ogr, fg, b, b_ref, b_row0, cols, gn, st_ref, msk_ref, eye, sub, right_lo,
                     sgn_lo):
    kk = 1.0 - fg
    qh = qr * _sigmoid(qr)
    v = vr

    def slabs(x):
        return [x[i * SUBLANES:(i + 1) * SUBLANES] for i in range(N_SLABS)]

    def row(r):
        return jnp.broadcast_to(b_ref[pl.ds(b_row0 + r, 1), cols], (SUBLANES, LANES))

    bs, qs, ks, fs = slabs(b), slabs(qh), slabs(kk), slabs(fg)
    sc = slabs(eye * jnp.sum(qh * kk, axis=-1, keepdims=True))
    for l in range(N_LEVELS):
        m = 1 << l
        zs = []
        for i in range(N_SLABS):
            base = i * SUBLANES
            if m >= SUBLANES:
                bnd = row(base - base % (2 * m) + m - 1)
                if base % (2 * m) >= m:
                    zs.append(qs[i] * jnp.exp2(bs[i] - bnd))
                else:
                    zs.append(ks[i] * jnp.exp2(bnd - bs[i]))
            elif l == 0:
                zs.append(jnp.where(right_lo[0], qs[i] * fs[i], ks[i]))
            else:
                bnd = row(base + m - 1)
                for pp in range(1, SUBLANES // (2 * m)):
                    bnd = jnp.where(sub >= pp * 2 * m, row(base + pp * 2 * m + m - 1), bnd)
                e = (bs[i] - bnd) * sgn_lo[l]
                zs.append(jnp.where(right_lo[l], qs[i], ks[i]) * jnp.exp2(e))
        z = jnp.concatenate(zs, axis=0)
        s_l = jnp.dot(z.astype(BF16), z.T.astype(BF16), preferred_element_type=F32)
        for i in range(N_SLABS):
            base = i * SUBLANES
            if m >= SUBLANES and base % (2 * m) < m:
                continue
            sc[i] = sc[i] + msk_ref[l, pl.ds(base, SUBLANES), :] * s_l[base:base + SUBLANES]
    scores = jnp.concatenate(sc, axis=0)
    o_intra = jnp.dot(scores.astype(BF16), v, preferred_element_type=F32)

    st = st_ref[...]
    q_in = (qh * jnp.exp2(b)).astype(BF16)
    o_inter = lax.dot_general(q_in, st.astype(BF16), _NT, preferred_element_type=F32)
    b_last = b_ref[pl.ds(b_row0 + CHUNK - 1, 1), cols]
    k_up = (kk * jnp.exp2(b_last - b)).astype(BF16)
    st_ref[...] = st * jnp.exp2(b_last) + lax.dot_general(v, k_up, _TN, preferred_element_type=F32)

    o = o_intra + o_inter
    o = o * lax.rsqrt(jnp.mean(o * o, axis=-1, keepdims=True) + RMS_EPS) * gn
    return (o * (ogr * _sigmoid(ogr))).astype(BF16)


def _conv_rows_group(ubuf_ref, r0, cols, w_ref, cb, ng, nb):
    base = r0 + (CONV_HALO - (CONV_KERNEL - 1))
    acc = jnp.broadcast_to(cb, (CONV_ROWS, LANES))
    for r in range(SUBLANES):
        taps = list(range(r, CONV_KERNEL, SUBLANES))
        xr = ubuf_ref[pl.ds(base + r, CONV_ROWS + (len(taps) - 1) * SUBLANES), :]
        for a, k in enumerate(taps):
            acc = acc + w_ref[pl.ds(k, 1), cols] * xr[a * SUBLANES:a * SUBLANES + CONV_ROWS, :]
    y = _layer_norm(acc, ng, nb, LN_EPS)
    return (y * _sigmoid(y)).astype(BF16)


def _mixer_kernel(n_tiles, tiles_per_seq,
                  a_ref, gate_ref, q_ref, f_ref, i_ref, og_ref, cw_ref, ccb_ref, cng_ref, cnb_ref,
                  lbl_ref, gn_ref, msk_ref, tri_ref, eye_ref, wu_ref, wd_ref,
                  x_ref, mu_ref, rs_ref, g0_ref, b0_ref, wout_ref, g1_ref, b1_ref,
                  h_ref, hb_ref, wub_ref, wdb_ref,
                  st_ref, b_ref, fg_ref, ubuf_ref, uo_ref, mix_ref):
    tt = TT_HGRN
    s = pl.program_id(0)
    seq_start = (jnp.minimum(s, n_tiles - 1) % tiles_per_seq) == 0
    slot_w = s % 2
    slot_r = 1 - slot_w

    @pl.when(s == 0)
    def _():
        uo_ref[...] = jnp.zeros(uo_ref.shape, BF16)

    @pl.when(seq_start)
    def _():
        st_ref[...] = jnp.zeros((HGRN_HEADS, HEAD_DIM, HEAD_DIM), F32)
        ubuf_ref[:, pl.ds(0, CONV_HALO), :] = jnp.zeros((CONV_GROUPS, CONV_HALO, LANES), F32)

    @pl.when(jnp.logical_not(seq_start))
    def _():
        ubuf_ref[:, pl.ds(0, CONV_HALO), :] = ubuf_ref[:, pl.ds(tt, CONV_HALO), :]

    for f in range(D_FF // TF_FFN):
        wub_ref[:, pl.ds(2 * f * TF_FFN, TF_FFN)] = wu_ref[:, pl.ds(f * TF_FFN, TF_FFN)].astype(BF16)
        wub_ref[:, pl.ds((2 * f + 1) * TF_FFN, TF_FFN)] = (
            wu_ref[:, pl.ds(D_FF + f * TF_FFN, TF_FFN)].astype(BF16))
    wdb_ref[...] = wd_ref[...].astype(BF16)

    rows = [lbl_ref[pl.ds(r, 1), :] for r in range(DEPTH + 1)]
    mx = functools.reduce(jnp.maximum, rows)
    ex = [jnp.exp(r - mx) for r in rows]
    lb_all = ex[0] / functools.reduce(lambda a, c: a + c, ex)
    gn_all = gn_ref[...]

    tri = tri_ref[...]
    for rb in range(tt // CUM_ROWS):
        crows = pl.ds(rb * CUM_ROWS, CUM_ROWS)
        fg = lb_all + (1.0 - lb_all) * _sigmoid(f_ref[crows, :].astype(F32))
        g2 = jnp.log(fg) * LOG2E
        g_hi = g2.astype(BF16)
        g_lo = (g2 - g_hi.astype(F32)).astype(BF16)
        b_ref[crows, :] = (jnp.dot(tri, g_hi, preferred_element_type=F32)
                           + jnp.dot(tri, g_lo, preferred_element_type=F32))
        fg_ref[crows, :] = fg

    sub = lax.broadcasted_iota(jnp.int32, (SUBLANES, LANES), 0)
    right_lo = [(sub % (2 << l)) >= (1 << l) for l in range(3)]
    sgn_lo = [jnp.where(r, 1.0, -1.0).astype(F32) for r in right_lo]
    eye = eye_ref[...]
    ccb_all = ccb_ref[...]
    cng_all = cng_ref[...]
    cnb_all = cnb_ref[...]

    n_units = (tt // CHUNK) * HGRN_HEADS
    out_cols = D_MODEL // OUT_PIECES
    n_ln = tt // LN_ROWS_INLINE
    ln_per_unit = -(-n_ln // (n_units - OUT_PIECES))

    for c in range(tt // CHUNK):
        r0 = c * CHUNK
        rws = pl.ds(r0, CHUNK)
        for h in range(HGRN_HEADS):
            cols = pl.ds(h * HEAD_DIM, HEAD_DIM)
            lane = slice(h * HEAD_DIM, (h + 1) * HEAD_DIM)
            ubuf_ref[h, pl.ds(CONV_HALO + r0, CHUNK), :] = (
                a_ref[rws, cols].astype(F32) * _sigmoid(gate_ref[rws, cols].astype(F32)))
            uo_ref[slot_w, rws, pl.ds(CONV_WIDTH + h * HEAD_DIM, HEAD_DIM)] = _hgrn_chunk_head(
                q_ref[rws, cols].astype(F32), i_ref[rws, cols], og_ref[rws, cols].astype(F32),
                fg_ref[rws, cols], b_ref[rws, cols], b_ref, r0, cols, gn_all[:, lane], st_ref.at[h],
                msk_ref, eye, sub, right_lo, sgn_lo)
            for j in range(CHUNK // CONV_ROWS):
                rr = r0 + j * CONV_ROWS
                uo_ref[slot_w, pl.ds(rr, CONV_ROWS), cols] = _conv_rows_group(
                    ubuf_ref.at[h], rr, cols, cw_ref, ccb_all[:, lane], cng_all[:, lane],
                    cnb_all[:, lane])

            unit = c * HGRN_HEADS + h
            if unit < OUT_PIECES:
                pc = pl.ds(unit * out_cols, out_cols)
                mix_ref[:, pc] = jnp.dot(uo_ref[slot_r], wout_ref[:, pc], preferred_element_type=F32)
            else:
                for r in range((unit - OUT_PIECES) * ln_per_unit,
                               min((unit - OUT_PIECES + 1) * ln_per_unit, n_ln)):
                    rows = pl.ds(r * LN_ROWS_INLINE, LN_ROWS_INLINE)
                    rep = D_MODEL // LANES
                    mu = jnp.tile(mu_ref[rows, :], (1, rep))
                    rs = jnp.tile(rs_ref[rows, :], (1, rep))
                    h0 = (x_ref[rows, :] - mu) * rs * g0_ref[...] + b0_ref[...]
                    y = ALPHA * h0 + mix_ref[rows, :]
                    h1 = _layer_norm(y, g1_ref[...], b1_ref[...], LN_EPS)
                    h_ref[rows, :] = h1
                    hb_ref[rows, :] = h1.astype(BF16)


def _mixer(proj, conv_w, conv_b, conv_ng, conv_nb, lb_logits, gn, w_up, w_down,
           x2, mu, rs, g0, b0, w_out_bf16, g1, b1, batch, seq):
    assert CONV_GROUPS == HGRN_HEADS and CONV_WIDTH == HGRN_WIDTH
    m = proj.shape[0]
    nt = seq // TT_HGRN
    n_steps = batch * nt
    msk, tri, eye = _hgrn_constants()
    col0 = 2 * CONV_WIDTH // HGRN_WIDTH
    up_rows = w_up.shape[0] // n_steps
    down_rows = w_down.shape[0] // n_steps
    assert up_rows * n_steps == w_up.shape[0] and down_rows * n_steps == w_down.shape[0]
    assert up_rows % 16 == 0 and down_rows % 16 == 0
    cur = lambda s: jnp.minimum(s, n_steps - 1)
    prev = lambda s: jnp.maximum(s - 1, 0)

    def col(n):
        return lambda s: (cur(s), n)

    slab = lambda s: (cur(s), 0)
    prow = lambda s: (prev(s), 0)
    const2 = lambda s: (0, 0)
    return pl.pallas_call(
        functools.partial(_mixer_kernel, n_steps, nt),
        grid=(n_steps + 1,),
        in_specs=[
            pl.BlockSpec((TT_HGRN, CONV_WIDTH), col(0)),
            pl.BlockSpec((TT_HGRN, CONV_WIDTH), col(1)),
            pl.BlockSpec((TT_HGRN, HGRN_WIDTH), col(col0)),
            pl.BlockSpec((TT_HGRN, HGRN_WIDTH), col(col0 + 1)),
            pl.BlockSpec((TT_HGRN, HGRN_WIDTH), col(col0 + 2)),
            pl.BlockSpec((TT_HGRN, HGRN_WIDTH), col(col0 + 3)),
            pl.BlockSpec((CONV_KERNEL, CONV_WIDTH), const2),
            pl.BlockSpec((1, CONV_WIDTH), const2),
            pl.BlockSpec((1, CONV_WIDTH), const2),
            pl.BlockSpec((1, CONV_WIDTH), const2),
            pl.BlockSpec((DEPTH + 1, HGRN_WIDTH), const2),
            pl.BlockSpec((1, HGRN_WIDTH), const2),
            pl.BlockSpec((N_LEVELS, CHUNK, CHUNK), lambda s: (0, 0, 0)),
            pl.BlockSpec((CUM_ROWS, CUM_ROWS), const2),
            pl.BlockSpec((CHUNK, CHUNK), const2),
            pl.BlockSpec((up_rows, w_up.shape[1]), slab),
            pl.BlockSpec((down_rows, w_down.shape[1]), slab),
            pl.BlockSpec((TT_HGRN, D_MODEL), prow),
            pl.BlockSpec((TT_HGRN, LANES), prow),
            pl.BlockSpec((TT_HGRN, LANES), prow),
            pl.BlockSpec((1, D_MODEL), const2),
            pl.BlockSpec((1, D_MODEL), const2),
            pl.BlockSpec((D_MODEL, D_MODEL), const2, pipeline_mode=pl.Buffered(1)),
            pl.BlockSpec((1, D_MODEL), const2),
            pl.BlockSpec((1, D_MODEL), const2),
        ],
        out_specs=[
            pl.BlockSpec((TT_HGRN, D_MODEL), prow),
            pl.BlockSpec((TT_HGRN, D_MODEL), prow),
            pl.BlockSpec((up_rows, w_up.shape[1]), slab),
            pl.BlockSpec((down_rows, w_down.shape[1]), slab),
        ],
        out_shape=[
            jax.ShapeDtypeStruct((m, D_MODEL), F32),
            jax.ShapeDtypeStruct((m, D_MODEL), BF16),
            jax.ShapeDtypeStruct(w_up.shape, BF16),
            jax.ShapeDtypeStruct(w_down.shape, BF16),
        ],
        scratch_shapes=[pltpu.VMEM((HGRN_HEADS, HEAD_DIM, HEAD_DIM), F32),
                        pltpu.VMEM((TT_HGRN, HGRN_WIDTH), F32),
                        pltpu.VMEM((TT_HGRN, HGRN_WIDTH), F32),
                        pltpu.VMEM((CONV_GROUPS, CONV_HALO + TT_HGRN, LANES), F32),
                        pltpu.VMEM((2, TT_HGRN, D_MODEL), BF16),
                        pltpu.VMEM((TT_HGRN, D_MODEL), F32)],
        compiler_params=pltpu.CompilerParams(
            dimension_semantics=("arbitrary",),
            vmem_limit_bytes=VMEM_LIMIT),
        name="mixer",
    )(proj, proj, proj, proj, proj, proj, conv_w, conv_b, conv_ng, conv_nb, lb_logits.astype(F32), gn,
      jnp.asarray(msk), jnp.asarray(tri, dtype=BF16), jnp.asarray(eye), w_up, w_down,
      x2, mu, rs, g0, b0, w_out_bf16, g1, b1)


def _ffn_kernel(tiles_per_seq, nf, hb_ref, hres_ref, wgv_ref, cw_ref, cb_ref, wd_ref, g2_ref,
                b2_ref, o_ref, acc_ref, g_ref, v_ref, tail_ref):
    s = pl.program_id(0)
    tm = TM_FFN
    n_slab = TF_FFN // LANES
    sb = jnp.maximum(s - 1, 0)
    i_b = sb // nf
    f_b = sb % nf

    @pl.when(s == 0)
    def _():
        acc_ref[...] = jnp.zeros(acc_ref.shape, F32)
        g_ref[...] = jnp.zeros(g_ref.shape, F32)
        v_ref[...] = jnp.zeros(v_ref.shape, F32)
        tail_ref[...] = jnp.zeros(tail_ref.shape, F32)

    def step(slot_a, slot_b):
        gv = jnp.dot(hb_ref[...], wgv_ref[...], preferred_element_type=F32)
        for j in range(n_slab):
            g_ref[slot_a, j, pl.ds(SUBLANES, tm), :] = gv[:, j * LANES:(j + 1) * LANES]
        v_ref[slot_a] = gv[:, TF_FFN:]

        seq_start = (i_b % tiles_per_seq) == 0
        g_ref[slot_b, :, pl.ds(0, SUBLANES), :] = jnp.where(seq_start, 0.0, tail_ref[f_b])
        tail_ref[f_b] = g_ref[slot_b, :, pl.ds(tm, SUBLANES), :]
        part = None
        slabs_per_k = FFN_K_COLS // LANES
        for kc in range(TF_FFN // FFN_K_COLS):
            acts = []
            for j in range(kc * slabs_per_k, (kc + 1) * slabs_per_k):
                cols = pl.ds(j * LANES, LANES)
                conv = (cb_ref[:, cols]
                        + cw_ref[pl.ds(2, 1), cols] * g_ref[slot_b, j, pl.ds(SUBLANES, tm), :]
                        + cw_ref[pl.ds(1, 1), cols] * g_ref[slot_b, j, pl.ds(SUBLANES - 1, tm), :]
                        + cw_ref[pl.ds(0, 1), cols] * g_ref[slot_b, j, pl.ds(SUBLANES - 2, tm), :])
                acts.append((conv * _sigmoid(conv) * v_ref[slot_b, :, cols]).astype(BF16))
            act = jnp.concatenate(acts, axis=1)
            d = jnp.dot(act, wd_ref[pl.ds(kc * FFN_K_COLS, FFN_K_COLS), :], preferred_element_type=F32)
            part = d if part is None else part + d
        acc_ref[...] += part

    @pl.when(s % 2 == 0)
    def _():
        step(0, 1)

    @pl.when(s % 2 == 1)
    def _():
        step(1, 0)

    @pl.when(jnp.logical_and(s > 0, f_b == nf - 1))
    def _():
        g2 = g2_ref[...]
        b2 = b2_ref[...]
        for r in range(tm // LN_ROWS_INLINE):
            rows = pl.ds(r * LN_ROWS_INLINE, LN_ROWS_INLINE)
            y = ALPHA * hres_ref[rows, :] + acc_ref[rows, :]
            o_ref[rows, :] = _layer_norm(y, g2, b2, LN_EPS)
            acc_ref[rows, :] = jnp.zeros((LN_ROWS_INLINE, D_MODEL), F32)


def _ffn(h1, h1_bf16, w_up_bf16, cw, cb, w_down_bf16, g2, b2, seq):
    m = h1.shape[0]
    nf = D_FF // TF_FFN
    n_steps = (m // TM_FFN) * nf
    a_step = lambda s: jnp.minimum(s, n_steps - 1)
    b_step = lambda s: jnp.maximum(s - 1, 0)
    return pl.pallas_call(
        functools.partial(_ffn_kernel, seq // TM_FFN, nf),
        grid=(n_steps + 1,),
        in_specs=[
            pl.BlockSpec((TM_FFN, D_MODEL), lambda s: (a_step(s) // nf, 0)),
            pl.BlockSpec((TM_FFN, D_MODEL), lambda s: (b_step(s) // nf, 0)),
            pl.BlockSpec((D_MODEL, 2 * TF_FFN), lambda s: (0, a_step(s) % nf)),
            pl.BlockSpec((FFN_KERNEL, TF_FFN), lambda s: (0, b_step(s) % nf)),
            pl.BlockSpec((1, TF_FFN), lambda s: (0, b_step(s) % nf)),
            pl.BlockSpec((TF_FFN, D_MODEL), lambda s: (b_step(s) % nf, 0)),
            pl.BlockSpec((1, D_MODEL), lambda s: (0, 0)),
            pl.BlockSpec((1, D_MODEL), lambda s: (0, 0)),
        ],
        out_specs=pl.BlockSpec((TM_FFN, D_MODEL), lambda s: (b_step(s) // nf, 0)),
        out_shape=jax.ShapeDtypeStruct((m, D_MODEL), F32),
        scratch_shapes=[
            pltpu.VMEM((TM_FFN, D_MODEL), F32),
            pltpu.VMEM((2, TF_FFN // LANES, SUBLANES + TM_FFN, LANES), F32),
            pltpu.VMEM((2, TM_FFN, TF_FFN), F32),
            pltpu.VMEM((nf, TF_FFN // LANES, SUBLANES, LANES), F32),
        ],
        compiler_params=pltpu.CompilerParams(
            dimension_semantics=("arbitrary",),
            vmem_limit_bytes=VMEM_LIMIT),
        name="conv_ffn_ln2",
    )(h1_bf16, h1, w_up_bf16, cw, cb, w_down_bf16, g2, b2)


def kernel(x, emb_ln_g, emb_ln_b, w_in, conv_w, conv_b, conv_norm_g, conv_norm_b, lb_logits,
           hgrn_norm_g, w_out, ln1_g, ln1_b, w_ffn_up, ffn_conv_w, ffn_conv_b, w_ffn_down,
           ln2_g, ln2_b):
    batch, seq, d = x.shape
    assert d == D_MODEL and w_in.shape[0] == DEPTH == 1
    assert seq % TT_HGRN == 0 and seq % TM_FFN == 0
    x2 = x.reshape(batch * seq, d)
    row = lambda a: a.reshape(1, -1).astype(F32)

    proj, mu0, rs0 = _inproj(x2, row(emb_ln_g), row(emb_ln_b), w_in[0])
    h1, h1_bf16, w_up_bf16, w_down_bf16 = _mixer(
        proj, conv_w[0], row(conv_b[0]), row(conv_norm_g[0]), row(conv_norm_b[0]), lb_logits,
        row(hgrn_norm_g[0]), w_ffn_up[0], w_ffn_down[0],
        x2, mu0, rs0, row(emb_ln_g), row(emb_ln_b), w_out[0].astype(BF16), row(ln1_g[0]),
        row(ln1_b[0]), batch, seq)
    out = _ffn(h1, h1_bf16, w_up_bf16, ffn_conv_w[0], row(ffn_conv_b[0]),
               w_down_bf16, row(ln2_g[0]), row(ln2_b[0]), seq)
    return out.reshape(batch, seq, d)
```

```python
import functools

import jax
import jax.numpy as jnp
import numpy as np
from jax import lax
from jax.experimental import pallas as pl
from jax.experimental.pallas import tpu as pltpu

F32 = jnp.float32
BF16 = jnp.bfloat16

D_MODEL = 2048
CONV_WIDTH = 1024
CONV_GROUPS = 8
CONV_KERNEL = 31
HGRN_WIDTH = 1024
HGRN_HEADS = 8
HEAD_DIM = 128
IN_PROJ_DIM = 2 * CONV_WIDTH + 4 * HGRN_WIDTH
D_FF = 5632
FFN_KERNEL = 3
LN_EPS = 1e-5
RMS_EPS = 1e-6
DEPTH = 1
ALPHA = (2.0 * DEPTH) ** 0.25
LOG2E = 1.4426950408889634

LANES = 128
SUBLANES = 8
VMEM_LIMIT = 56 * 1024 * 1024

TM_IN, TN_IN = 1024, 1024
IN_PIECE_ROWS = 256
CONV_HALO = 32
TT_HGRN = 256
OUT_PIECES = 8
CHUNK = 128
N_LEVELS = 7
CUM_ROWS = 256
CONV_ROWS = 64
N_SLABS = CHUNK // SUBLANES
TM_FFN, TF_FFN = 512, 512
FFN_K_COLS = 256
LN_ROWS = 64
LN_ROWS_INLINE = 16

_NT = (((1,), (1,)), ((), ()))
_TN = (((0,), (0,)), ((), ()))


def _ln_stats(x, eps):
    mu = jnp.mean(x, axis=-1, keepdims=True)
    xc = x - mu
    var = jnp.mean(xc * xc, axis=-1, keepdims=True)
    return mu, lax.rsqrt(var + eps), xc


def _layer_norm(x, g, b, eps):
    _, rs, xc = _ln_stats(x, eps)
    return xc * rs * g + b


def _sigmoid(x):
    return 1.0 / (1.0 + jnp.exp2(x * (-LOG2E)))


def _inproj_kernel(x_ref, g_ref, b_ref, w_ref, o_ref, mu_ref, rs_ref, xn_ref):
    @pl.when(pl.program_id(1) == 0)
    def _():
        g = g_ref[...]
        b = b_ref[...]
        wb = w_ref[...].astype(BF16)
        for p in range(TM_IN // IN_PIECE_ROWS):
            for r in range(IN_PIECE_ROWS // LN_ROWS_INLINE):
                rows = pl.ds(p * IN_PIECE_ROWS + r * LN_ROWS_INLINE, LN_ROWS_INLINE)
                mu, rs, xc = _ln_stats(x_ref[rows, :], LN_EPS)
                mu_ref[rows, :] = jnp.broadcast_to(mu, (LN_ROWS_INLINE, LANES))
                rs_ref[rows, :] = jnp.broadcast_to(rs, (LN_ROWS_INLINE, LANES))
                xn_ref[rows, :] = (xc * rs * g + b).astype(BF16)
            prows = pl.ds(p * IN_PIECE_ROWS, IN_PIECE_ROWS)
            o_ref[prows, :] = jnp.dot(xn_ref[prows, :], wb, preferred_element_type=F32).astype(BF16)

    @pl.when(pl.program_id(1) != 0)
    def _():
        o_ref[...] = jnp.dot(xn_ref[...], w_ref[...].astype(BF16),
                             preferred_element_type=F32).astype(BF16)


def _inproj(x2, g, b, w_f32):
    m = x2.shape[0]
    return pl.pallas_call(
        _inproj_kernel,
        grid=(m // TM_IN, IN_PROJ_DIM // TN_IN),
        in_specs=[
            pl.BlockSpec((TM_IN, D_MODEL), lambda i, j: (i, 0)),
            pl.BlockSpec((1, D_MODEL), lambda i, j: (0, 0)),
            pl.BlockSpec((1, D_MODEL), lambda i, j: (0, 0)),
            pl.BlockSpec((D_MODEL, TN_IN), lambda i, j: (0, j)),
        ],
        out_specs=[pl.BlockSpec((TM_IN, TN_IN), lambda i, j: (i, j)),
                   pl.BlockSpec((TM_IN, LANES), lambda i, j: (i, 0)),
                   pl.BlockSpec((TM_IN, LANES), lambda i, j: (i, 0))],
        out_shape=[jax.ShapeDtypeStruct((m, IN_PROJ_DIM), BF16),
                   jax.ShapeDtypeStruct((m, LANES), F32),
                   jax.ShapeDtypeStruct((m, LANES), F32)],
        scratch_shapes=[pltpu.VMEM((TM_IN, D_MODEL), BF16)],
        compiler_params=pltpu.CompilerParams(
            dimension_semantics=("arbitrary", "arbitrary"),
            vmem_limit_bytes=VMEM_LIMIT),
        name="ln_inproj",
    )(x2, g, b, w_f32)


def _hgrn_constants():
    t = np.arange(CHUNK)
    msk = np.zeros((N_LEVELS, CHUNK, CHUNK), np.float32)
    for l in range(N_LEVELS):
        m = 1 << l
        right = (t % (2 * m)) >= m
        same = (t[:, None] // (2 * m)) == (t[None, :] // (2 * m))
        msk[l] = (same & right[:, None] & (~right)[None, :]).astype(np.float32)
    r = np.arange(CUM_ROWS)
    tri = ((r[:, None] >= r[None, :]) & (r[:, None] // CHUNK == r[None, :] // CHUNK)).astype(np.float32)
    eye = np.eye(CHUNK, dtype=np.float32)
    return msk, tri, eye


def _hgrn_chunk_head(qr, vr, ogr, fg, b, b_ref, b_row0, cols, gn, st_ref, msk_ref, eye, sub, right_lo,
                     sgn_lo):
    kk = 1.0 - fg
    qh = qr * _sigmoid(qr)
    v = vr

    def slabs(x):
        return [x[i * SUBLANES:(i + 1) * SUBLANES] for i in range(N_SLABS)]

    def row(r):
        return jnp.broadcast_to(b_ref[pl.ds(b_row0 + r, 1), cols], (SUBLANES, LANES))

    bs, qs, ks, fs = slabs(b), slabs(qh), slabs(kk), slabs(fg)
    sc = slabs(eye * jnp.sum(qh * kk, axis=-1, keepdims=True))
    for l in range(N_LEVELS):
        m = 1 << l
        zs = []
        for i in range(N_SLABS):
            base = i * SUBLANES
            if m >= SUBLANES:
                bnd = row(base - base % (2 * m) + m - 1)
                if base % (2 * m) >= m:
                    zs.append(qs[i] * jnp.exp2(bs[i] - bnd))
                else:
                    zs.append(ks[i] * jnp.exp2(bnd - bs[i]))
            elif l == 0:
                zs.append(jnp.where(right_lo[0], qs[i] * fs[i], ks[i]))
            else:
                bnd = row(base + m - 1)
                for pp in range(1, SUBLANES // (2 * m)):
                    bnd = jnp.where(sub >= pp * 2 * m, row(base + pp * 2 * m + m - 1), bnd)
                e = (bs[i] - bnd) * sgn_lo[l]
                zs.append(jnp.where(right_lo[l], qs[i], ks[i]) * jnp.exp2(e))
        z = jnp.concatenate(zs, axis=0)
        zb = z.astype(BF16)
        s_l = jnp.dot(zb, zb.T, preferred_element_type=F32)
        for i in range(N_SLABS):
            base = i * SUBLANES
            if m >= SUBLANES and base % (2 * m) < m:
                continue
            sc[i] = sc[i] + msk_ref[l, pl.ds(base, SUBLANES), :] * s_l[base:base + SUBLANES]
    scores = jnp.concatenate(sc, axis=0)
    o_intra = jnp.dot(scores.astype(BF16), v, preferred_element_type=F32)

    st = st_ref[...]
    q_in = (qh * jnp.exp2(b)).astype(BF16)
    o_inter = lax.dot_general(q_in, st.astype(BF16), _NT, preferred_element_type=F32)
    b_last = b_ref[pl.ds(b_row0 + CHUNK - 1, 1), cols]
    k_up = (kk * jnp.exp2(b_last - b)).astype(BF16)
    st_ref[...] = st * jnp.exp2(b_last) + lax.dot_general(v, k_up, _TN, preferred_element_type=F32)

    o = o_intra + o_inter
    o = o * lax.rsqrt(jnp.mean(o * o, axis=-1, keepdims=True) + RMS_EPS) * gn
    return (o * (ogr * _sigmoid(ogr))).astype(BF16)


def _conv_rows_group(ubuf_ref, r0, cols, w_ref, cb, ng, nb):
    base = r0 + (CONV_HALO - (CONV_KERNEL - 1))
    acc = jnp.broadcast_to(cb, (CONV_ROWS, LANES))
    for r in range(SUBLANES):
        taps = list(range(r, CONV_KERNEL, SUBLANES))
        xr = ubuf_ref[pl.ds(base + r, CONV_ROWS + (len(taps) - 1) * SUBLANES), :]
        for a, k in enumerate(taps):
            acc = acc + w_ref[pl.ds(k, 1), cols] * xr[a * SUBLANES:a * SUBLANES + CONV_ROWS, :]
    y = _layer_norm(acc, ng, nb, LN_EPS)
    return (y * _sigmoid(y)).astype(BF16)


def _mixer_kernel(n_tiles, tiles_per_seq,
                  a_ref, gate_ref, q_ref, f_ref, i_ref, og_ref, cw_ref, ccb_ref, cng_ref, cnb_ref,
                  lbl_ref, gn_ref, msk_ref, tri_ref, eye_ref, wu_ref, wd_ref,
                  x_ref, mu_ref, rs_ref, g0_ref, b0_ref, wout_ref, g1_ref, b1_ref,
                  h_ref, hb_ref, wub_ref, wdb_ref,
                  st_ref, b_ref, fg_ref, ubuf_ref, uo_ref, mix_ref):
    tt = TT_HGRN
    s = pl.program_id(0)
    seq_start = (jnp.minimum(s, n_tiles - 1) % tiles_per_seq) == 0
    slot_w = s % 2
    slot_r = 1 - slot_w

    @pl.when(s == 0)
    def _():
        uo_ref[...] = jnp.zeros(uo_ref.shape, BF16)

    @pl.when(seq_start)
    def _():
        st_ref[...] = jnp.zeros((HGRN_HEADS, HEAD_DIM, HEAD_DIM), F32)
        ubuf_ref[:, pl.ds(0, CONV_HALO), :] = jnp.zeros((CONV_GROUPS, CONV_HALO, LANES), F32)

    @pl.when(jnp.logical_not(seq_start))
    def _():
        ubuf_ref[:, pl.ds(0, CONV_HALO), :] = ubuf_ref[:, pl.ds(tt, CONV_HALO), :]

    for f in range(D_FF // TF_FFN):
        wub_ref[:, pl.ds(2 * f * TF_FFN, TF_FFN)] = wu_ref[:, pl.ds(f * TF_FFN, TF_FFN)].astype(BF16)
        wub_ref[:, pl.ds((2 * f + 1) * TF_FFN, TF_FFN)] = (
            wu_ref[:, pl.ds(D_FF + f * TF_FFN, TF_FFN)].astype(BF16))
    wdb_ref[...] = wd_ref[...].astype(BF16)

    rows = [lbl_ref[pl.ds(r, 1), :] for r in range(DEPTH + 1)]
    mx = functools.reduce(jnp.maximum, rows)
    ex = [jnp.exp(r - mx) for r in rows]
    lb_all = ex[0] / functools.reduce(lambda a, c: a + c, ex)
    gn_all = gn_ref[...]

    tri = tri_ref[...]
    for rb in range(tt // CUM_ROWS):
        crows = pl.ds(rb * CUM_ROWS, CUM_ROWS)
        fg = lb_all + (1.0 - lb_all) * _sigmoid(f_ref[crows, :].astype(F32))
        g2 = jnp.log(fg) * LOG2E
        g_hi = g2.astype(BF16)
        g_lo = (g2 - g_hi.astype(F32)).astype(BF16)
        b_ref[crows, :] = (jnp.dot(tri, g_hi, preferred_element_type=F32)
                           + jnp.dot(tri, g_lo, preferred_element_type=F32))
        fg_ref[crows, :] = fg

    sub = lax.broadcasted_iota(jnp.int32, (SUBLANES, LANES), 0)
    right_lo = [(sub % (2 << l)) >= (1 << l) for l in range(3)]
    sgn_lo = [jnp.where(r, 1.0, -1.0).astype(F32) for r in right_lo]
    eye = eye_ref[...]
    ccb_all = ccb_ref[...]
    cng_all = cng_ref[...]
    cnb_all = cnb_ref[...]

    n_units = (tt // CHUNK) * HGRN_HEADS
    out_cols = D_MODEL // OUT_PIECES
    n_ln = tt // LN_ROWS_INLINE
    ln_per_unit = -(-n_ln // (n_units - OUT_PIECES))

    for c in range(tt // CHUNK):
        r0 = c * CHUNK
        rws = pl.ds(r0, CHUNK)
        for h in range(HGRN_HEADS):
            cols = pl.ds(h * HEAD_DIM, HEAD_DIM)
            lane = slice(h * HEAD_DIM, (h + 1) * HEAD_DIM)
            ubuf_ref[h, pl.ds(CONV_HALO + r0, CHUNK), :] = (
                a_ref[rws, cols].astype(F32) * _sigmoid(gate_ref[rws, cols].astype(F32)))
            uo_ref[slot_w, rws, pl.ds(CONV_WIDTH + h * HEAD_DIM, HEAD_DIM)] = _hgrn_chunk_head(
                q_ref[rws, cols].astype(F32), i_ref[rws, cols], og_ref[rws, cols].astype(F32),
                fg_ref[rws, cols], b_ref[rws, cols], b_ref, r0, cols, gn_all[:, lane], st_ref.at[h],
                msk_ref, eye, sub, right_lo, sgn_lo)
            for j in range(CHUNK // CONV_ROWS):
                rr = r0 + j * CONV_ROWS
                uo_ref[slot_w, pl.ds(rr, CONV_ROWS), cols] = _conv_rows_group(
                    ubuf_ref.at[h], rr, cols, cw_ref, ccb_all[:, lane], cng_all[:, lane],
                    cnb_all[:, lane])

            unit = c * HGRN_HEADS + h
            if unit < OUT_PIECES:
                pc = pl.ds(unit * out_cols, out_cols)
                mix_ref[:, pc] = jnp.dot(uo_ref[slot_r], wout_ref[:, pc], preferred_element_type=F32)
            else:
                for r in range((unit - OUT_PIECES) * ln_per_unit,
                               min((unit - OUT_PIECES + 1) * ln_per_unit, n_ln)):
                    rows = pl.ds(r * LN_ROWS_INLINE, LN_ROWS_INLINE)
                    rep = D_MODEL // LANES
                    mu = jnp.tile(mu_ref[rows, :], (1, rep))
                    rs = jnp.tile(rs_ref[rows, :], (1, rep))
                    h0 = (x_ref[rows, :] - mu) * rs * g0_ref[...] + b0_ref[...]
                    y = ALPHA * h0 + mix_ref[rows, :]
                    h1 = _layer_norm(y, g1_ref[...], b1_ref[...], LN_EPS)
                    h_ref[rows, :] = h1
                    hb_ref[rows, :] = h1.astype(BF16)


def _mixer(proj, conv_w, conv_b, conv_ng, conv_nb, lb_logits, gn, w_up, w_down,
           x2, mu, rs, g0, b0, w_out_bf16, g1, b1, batch, seq):
    assert CONV_GROUPS == HGRN_HEADS and CONV_WIDTH == HGRN_WIDTH
    m = proj.shape[0]
    nt = seq // TT_HGRN
    n_steps = batch * nt
    msk, tri, eye = _hgrn_constants()
    col0 = 2 * CONV_WIDTH // HGRN_WIDTH
    up_rows = w_up.shape[0] // n_steps
    down_rows = w_down.shape[0] // n_steps
    assert up_rows * n_steps == w_up.shape[0] and down_rows * n_steps == w_down.shape[0]
    assert up_rows % 16 == 0 and down_rows % 16 == 0
    cur = lambda s: jnp.minimum(s, n_steps - 1)
    prev = lambda s: jnp.maximum(s - 1, 0)

    def col(n):
        return lambda s: (cur(s), n)

    slab = lambda s: (cur(s), 0)
    prow = lambda s: (prev(s), 0)
    const2 = lambda s: (0, 0)
    return pl.pallas_call(
        functools.partial(_mixer_kernel, n_steps, nt),
        grid=(n_steps + 1,),
        in_specs=[
            pl.BlockSpec((TT_HGRN, CONV_WIDTH), col(0)),
            pl.BlockSpec((TT_HGRN, CONV_WIDTH), col(1)),
            pl.BlockSpec((TT_HGRN, HGRN_WIDTH), col(col0)),
            pl.BlockSpec((TT_HGRN, HGRN_WIDTH), col(col0 + 1)),
            pl.BlockSpec((TT_HGRN, HGRN_WIDTH), col(col0 + 2)),
            pl.BlockSpec((TT_HGRN, HGRN_WIDTH), col(col0 + 3)),
            pl.BlockSpec((CONV_KERNEL, CONV_WIDTH), const2),
            pl.BlockSpec((1, CONV_WIDTH), const2),
            pl.BlockSpec((1, CONV_WIDTH), const2),
            pl.BlockSpec((1, CONV_WIDTH), const2),
            pl.BlockSpec((DEPTH + 1, HGRN_WIDTH), const2),
            pl.BlockSpec((1, HGRN_WIDTH), const2),
            pl.BlockSpec((N_LEVELS, CHUNK, CHUNK), lambda s: (0, 0, 0)),
            pl.BlockSpec((CUM_ROWS, CUM_ROWS), const2),
            pl.BlockSpec((CHUNK, CHUNK), const2),
            pl.BlockSpec((up_rows, w_up.shape[1]), slab),
            pl.BlockSpec((down_rows, w_down.shape[1]), slab),
            pl.BlockSpec((TT_HGRN, D_MODEL), prow),
            pl.BlockSpec((TT_HGRN, LANES), prow),
            pl.BlockSpec((TT_HGRN, LANES), prow),
            pl.BlockSpec((1, D_MODEL), const2),
            pl.BlockSpec((1, D_MODEL), const2),
            pl.BlockSpec((D_MODEL, D_MODEL), const2, pipeline_mode=pl.Buffered(1)),
            pl.BlockSpec((1, D_MODEL), const2),
            pl.BlockSpec((1, D_MODEL), const2),
        ],
        out_specs=[
            pl.BlockSpec((TT_HGRN, D_MODEL), prow),
            pl.BlockSpec((TT_HGRN, D_MODEL), prow),
            pl.BlockSpec((up_rows, w_up.shape[1]), slab),
            pl.BlockSpec((down_rows, w_down.shape[1]), slab),
        ],
        out_shape=[
            jax.ShapeDtypeStruct((m, D_MODEL), F32),
            jax.ShapeDtypeStruct((m, D_MODEL), BF16),
            jax.ShapeDtypeStruct(w_up.shape, BF16),
            jax.ShapeDtypeStruct(w_down.shape, BF16),
        ],
        scratch_shapes=[pltpu.VMEM((HGRN_HEADS, HEAD_DIM, HEAD_DIM), F32),
                        pltpu.VMEM((TT_HGRN, HGRN_WIDTH), F32),
                        pltpu.VMEM((TT_HGRN, HGRN_WIDTH), F32),
                        pltpu.VMEM((CONV_GROUPS, CONV_HALO + TT_HGRN, LANES), F32),
                        pltpu.VMEM((2, TT_HGRN, D_MODEL), BF16),
                        pltpu.VMEM((TT_HGRN, D_MODEL), F32)],
        compiler_params=pltpu.CompilerParams(
            dimension_semantics=("arbitrary",),
            vmem_limit_bytes=VMEM_LIMIT),
        name="mixer",
    )(proj, proj, proj, proj, proj, proj, conv_w, conv_b, conv_ng, conv_nb, lb_logits.astype(F32), gn,
      jnp.asarray(msk), jnp.asarray(tri, dtype=BF16), jnp.asarray(eye), w_up, w_down,
      x2, mu, rs, g0, b0, w_out_bf16, g1, b1)


def _ffn_kernel(tiles_per_seq, nf, hb_ref, hres_ref, wgv_ref, cw_ref, cb_ref, wd_ref, g2_ref,
                b2_ref, o_ref, acc_ref, g_ref, v_ref, tail_ref):
    s = pl.program_id(0)
    tm = TM_FFN
    n_slab = TF_FFN // LANES
    sb = jnp.maximum(s - 1, 0)
    i_b = sb // nf
    f_b = sb % nf

    @pl.when(s == 0)
    def _():
        acc_ref[...] = jnp.zeros(acc_ref.shape, F32)
        g_ref[...] = jnp.zeros(g_ref.shape, F32)
        v_ref[...] = jnp.zeros(v_ref.shape, F32)
        tail_ref[...] = jnp.zeros(tail_ref.shape, F32)

    def step(slot_a, slot_b):
        gv = jnp.dot(hb_ref[...], wgv_ref[...], preferred_element_type=F32)
        for j in range(n_slab):
            g_ref[slot_a, j, pl.ds(SUBLANES, tm), :] = gv[:, j * LANES:(j + 1) * LANES]
        v_ref[slot_a] = gv[:, TF_FFN:]

        seq_start = (i_b % tiles_per_seq) == 0
        g_ref[slot_b, :, pl.ds(0, SUBLANES), :] = jnp.where(seq_start, 0.0, tail_ref[f_b])
        tail_ref[f_b] = g_ref[slot_b, :, pl.ds(tm, SUBLANES), :]
        part = None
        slabs_per_k = FFN_K_COLS // LANES
        for kc in range(TF_FFN // FFN_K_COLS):
            acts = []
            for j in range(kc * slabs_per_k, (kc + 1) * slabs_per_k):
                cols = pl.ds(j * LANES, LANES)
                conv = (cb_ref[:, cols]
                        + cw_ref[pl.ds(2, 1), cols] * g_ref[slot_b, j, pl.ds(SUBLANES, tm), :]
                        + cw_ref[pl.ds(1, 1), cols] * g_ref[slot_b, j, pl.ds(SUBLANES - 1, tm), :]
                        + cw_ref[pl.ds(0, 1), cols] * g_ref[slot_b, j, pl.ds(SUBLANES - 2, tm), :])
                acts.append((conv * _sigmoid(conv) * v_ref[slot_b, :, cols]).astype(BF16))
            act = jnp.concatenate(acts, axis=1)
            d = jnp.dot(act, wd_ref[pl.ds(kc * FFN_K_COLS, FFN_K_COLS), :], preferred_element_type=F32)
            part = d if part is None else part + d
        acc_ref[...] += part

    @pl.when(s % 2 == 0)
    def _():
        step(0, 1)

    @pl.when(s % 2 == 1)
    def _():
        step(1, 0)

    @pl.when(jnp.logical_and(s > 0, f_b == nf - 1))
    def _():
        g2 = g2_ref[...]
        b2 = b2_ref[...]
        for r in range(tm // LN_ROWS_INLINE):
            rows = pl.ds(r * LN_ROWS_INLINE, LN_ROWS_INLINE)
            y = ALPHA * hres_ref[rows, :] + acc_ref[rows, :]
            o_ref[rows, :] = _layer_norm(y, g2, b2, LN_EPS)
            acc_ref[rows, :] = jnp.zeros((LN_ROWS_INLINE, D_MODEL), F32)


def _ffn(h1, h1_bf16, w_up_bf16, cw, cb, w_down_bf16, g2, b2, seq):
    m = h1.shape[0]
    nf = D_FF // TF_FFN
    n_steps = (m // TM_FFN) * nf
    a_step = lambda s: jnp.minimum(s, n_steps - 1)
    b_step = lambda s: jnp.maximum(s - 1, 0)
    return pl.pallas_call(
        functools.partial(_ffn_kernel, seq // TM_FFN, nf),
        grid=(n_steps + 1,),
        in_specs=[
            pl.BlockSpec((TM_FFN, D_MODEL), lambda s: (a_step(s) // nf, 0)),
            pl.BlockSpec((TM_FFN, D_MODEL), lambda s: (b_step(s) // nf, 0)),
            pl.BlockSpec((D_MODEL, 2 * TF_FFN), lambda s: (0, a_step(s) % nf)),
            pl.BlockSpec((FFN_KERNEL, TF_FFN), lambda s: (0, b_step(s) % nf)),
            pl.BlockSpec((1, TF_FFN), lambda s: (0, b_step(s) % nf)),
            pl.BlockSpec((TF_FFN, D_MODEL), lambda s: (b_step(s) % nf, 0)),
            pl.BlockSpec((1, D_MODEL), lambda s: (0, 0)),
            pl.BlockSpec((1, D_MODEL), lambda s: (0, 0)),
        ],
        out_specs=pl.BlockSpec((TM_FFN, D_MODEL), lambda s: (b_step(s) // nf, 0)),
        out_shape=jax.ShapeDtypeStruct((m, D_MODEL), F32),
        scratch_shapes=[
            pltpu.VMEM((TM_FFN, D_MODEL), F32),
            pltpu.VMEM((2, TF_FFN // LANES, SUBLANES + TM_FFN, LANES), F32),
            pltpu.VMEM((2, TM_FFN, TF_FFN), F32),
            pltpu.VMEM((nf, TF_FFN // LANES, SUBLANES, LANES), F32),
        ],
        compiler_params=pltpu.CompilerParams(
            dimension_semantics=("arbitrary",),
            vmem_limit_bytes=VMEM_LIMIT),
        name="conv_ffn_ln2",
    )(h1_bf16, h1, w_up_bf16, cw, cb, w_down_bf16, g2, b2)


def kernel(x, emb_ln_g, emb_ln_b, w_in, conv_w, conv_b, conv_norm_g, conv_norm_b, lb_logits,
           hgrn_norm_g, w_out, ln1_g, ln1_b, w_ffn_up, ffn_conv_w, ffn_conv_b, w_ffn_down,
           ln2_g, ln2_b):
    batch, seq, d = x.shape
    assert d == D_MODEL and w_in.shape[0] == DEPTH == 1
    assert seq % TT_HGRN == 0 and seq % TM_FFN == 0
    x2 = x.reshape(batch * seq, d)
    row = lambda a: a.reshape(1, -1).astype(F32)

    proj, mu0, rs0 = _inproj(x2, row(emb_ln_g), row(emb_ln_b), w_in[0])
    h1, h1_bf16, w_up_bf16, w_down_bf16 = _mixer(
        proj, conv_w[0], row(conv_b[0]), row(conv_norm_g[0]), row(conv_norm_b[0]), lb_logits,
        row(hgrn_norm_g[0]), w_ffn_up[0], w_ffn_down[0],
        x2, mu0, rs0, row(emb_ln_g), row(emb_ln_b), w_out[0].astype(BF16), row(ln1_g[0]),
        row(ln1_b[0]), batch, seq)
    out = _ffn(h1, h1_bf16, w_up_bf16, ffn_conv_w[0], row(ffn_conv_b[0]),
               w_down_bf16, row(ln2_g[0]), row(ln2_b[0]), seq)
    return out.reshape(batch, seq, d)
```

```python
import functools

import jax
import jax.numpy as jnp
import numpy as np
from jax import lax
from jax.experimental import pallas as pl
from jax.experimental.pallas import tpu as pltpu

F32 = jnp.float32
BF16 = jnp.bfloat16

D_MODEL = 2048
CONV_WIDTH = 1024
CONV_GROUPS = 8
CONV_KERNEL = 31
HGRN_WIDTH = 1024
HGRN_HEADS = 8
HEAD_DIM = 128
IN_PROJ_DIM = 2 * CONV_WIDTH + 4 * HGRN_WIDTH
D_FF = 5632
FFN_KERNEL = 3
LN_EPS = 1e-5
RMS_EPS = 1e-6
DEPTH = 1
ALPHA = (2.0 * DEPTH) ** 0.25
LOG2E = 1.4426950408889634

LANES = 128
SUBLANES = 8
VMEM_LIMIT = 56 * 1024 * 1024

TM_IN, TN_IN = 1024, 1024
IN_PIECE_ROWS = 256
CONV_HALO = 32
TT_HGRN = 256
OUT_PIECES = 8
CHUNK = 128
N_LEVELS = 7
CUM_ROWS = 256
CONV_ROWS = 64
N_SLABS = CHUNK // SUBLANES
TM_FFN, TF_FFN = 512, 512
FFN_K_COLS = 256
LN_ROWS_INLINE = 16

_NT = (((1,), (1,)), ((), ()))
_TN = (((0,), (0,)), ((), ()))


def _ln_stats(x, eps):
    mu = jnp.mean(x, axis=-1, keepdims=True)
    xc = x - mu
    var = jnp.mean(xc * xc, axis=-1, keepdims=True)
    return mu, lax.rsqrt(var + eps), xc


def _layer_norm(x, g, b, eps):
    _, rs, xc = _ln_stats(x, eps)
    return xc * rs * g + b


def _sigmoid(x):
    return 1.0 / (1.0 + jnp.exp2(x * (-LOG2E)))


def _inproj_kernel(x_ref, g_ref, b_ref, w_ref, o_ref, mu_ref, rs_ref, xn_ref):
    @pl.when(pl.program_id(1) == 0)
    def _():
        g = g_ref[...]
        b = b_ref[...]
        wb = w_ref[...].astype(BF16)
        for p in range(TM_IN // IN_PIECE_ROWS):
            for r in range(IN_PIECE_ROWS // LN_ROWS_INLINE):
                rows = pl.ds(p * IN_PIECE_ROWS + r * LN_ROWS_INLINE, LN_ROWS_INLINE)
                mu, rs, xc = _ln_stats(x_ref[rows, :], LN_EPS)
                mu_ref[rows, :] = jnp.broadcast_to(mu, (LN_ROWS_INLINE, LANES))
                rs_ref[rows, :] = jnp.broadcast_to(rs, (LN_ROWS_INLINE, LANES))
                xn_ref[rows, :] = (xc * rs * g + b).astype(BF16)
            prows = pl.ds(p * IN_PIECE_ROWS, IN_PIECE_ROWS)
            o_ref[prows, :] = jnp.dot(xn_ref[prows, :], wb, preferred_element_type=F32).astype(BF16)

    @pl.when(pl.program_id(1) != 0)
    def _():
        o_ref[...] = jnp.dot(xn_ref[...], w_ref[...].astype(BF16),
                             preferred_element_type=F32).astype(BF16)


def _inproj(x2, g, b, w_f32):
    m = x2.shape[0]
    return pl.pallas_call(
        _inproj_kernel,
        grid=(m // TM_IN, IN_PROJ_DIM // TN_IN),
        in_specs=[
            pl.BlockSpec((TM_IN, D_MODEL), lambda i, j: (i, 0)),
            pl.BlockSpec((1, D_MODEL), lambda i, j: (0, 0)),
            pl.BlockSpec((1, D_MODEL), lambda i, j: (0, 0)),
            pl.BlockSpec((D_MODEL, TN_IN), lambda i, j: (0, j)),
        ],
        out_specs=[pl.BlockSpec((TM_IN, TN_IN), lambda i, j: (i, j)),
                   pl.BlockSpec((TM_IN, LANES), lambda i, j: (i, 0)),
                   pl.BlockSpec((TM_IN, LANES), lambda i, j: (i, 0))],
        out_shape=[jax.ShapeDtypeStruct((m, IN_PROJ_DIM), BF16),
                   jax.ShapeDtypeStruct((m, LANES), F32),
                   jax.ShapeDtypeStruct((m, LANES), F32)],
        scratch_shapes=[pltpu.VMEM((TM_IN, D_MODEL), BF16)],
        compiler_params=pltpu.CompilerParams(
            dimension_semantics=("arbitrary", "arbitrary"),
            vmem_limit_bytes=VMEM_LIMIT),
        name="ln_inproj",
    )(x2, g, b, w_f32)


def _hgrn_constants():
    t = np.arange(CHUNK)
    msk = np.zeros((N_LEVELS, CHUNK, CHUNK), np.float32)
    for l in range(N_LEVELS):
        m = 1 << l
        right = (t % (2 * m)) >= m
        same = (t[:, None] // (2 * m)) == (t[None, :] // (2 * m))
        msk[l] = (same & right[:, None] & (~right)[None, :]).astype(np.float32)
    r = np.arange(CUM_ROWS)
    tri = ((r[:, None] >= r[None, :]) & (r[:, None] // CHUNK == r[None, :] // CHUNK)).astype(np.float32)
    eye = np.eye(CHUNK, dtype=np.float32)
    return msk, tri, eye


def _hgrn_chunk_head(qr, vr, ogr, fg, b, b_ref, b_row0, cols, gn, st_ref, msk_ref, eye, sub, right_lo,
                     sgn_lo):
    kk = 1.0 - fg
    qh = qr * _sigmoid(qr)
    v = vr

    def slabs(x):
        return [x[i * SUBLANES:(i + 1) * SUBLANES] for i in range(N_SLABS)]

    def row(r):
        return jnp.broadcast_to(b_ref[pl.ds(b_row0 + r, 1), cols], (SUBLANES, LANES))

    bs, qs, ks, fs = slabs(b), slabs(qh), slabs(kk), slabs(fg)
    sc = slabs(eye * jnp.sum(qh * kk, axis=-1, keepdims=True))
    for l in range(N_LEVELS):
        m = 1 << l
        zs = []
        for i in range(N_SLABS):
            base = i * SUBLANES
            if m >= SUBLANES:
                bnd = row(base - base % (2 * m) + m - 1)
                if base % (2 * m) >= m:
                    zs.append(qs[i] * jnp.exp2(bs[i] - bnd))
                else:
                    zs.append(ks[i] * jnp.exp2(bnd - bs[i]))
            elif l == 0:
                zs.append(jnp.where(right_lo[0], qs[i] * fs[i], ks[i]))
            else:
                bnd = row(base + m - 1)
                for pp in range(1, SUBLANES // (2 * m)):
                    bnd = jnp.where(sub >= pp * 2 * m, row(base + pp * 2 * m + m - 1), bnd)
                e = (bs[i] - bnd) * sgn_lo[l]
                zs.append(jnp.where(right_lo[l], qs[i], ks[i]) * jnp.exp2(e))
        z = jnp.concatenate(zs, axis=0)
        zb = z.astype(BF16)
        s_l = jnp.dot(zb, zb.T, preferred_element_type=F32)
        for i in range(N_SLABS):
            base = i * SUBLANES
            if m >= SUBLANES and base % (2 * m) < m:
                continue
            sc[i] = sc[i] + msk_ref[l, pl.ds(base, SUBLANES), :] * s_l[base:base + SUBLANES]
    scores = jnp.concatenate(sc, axis=0)
    o_intra = jnp.dot(scores.astype(BF16), v, preferred_element_type=F32)

    st = st_ref[...]
    q_in = (qh * jnp.exp2(b)).astype(BF16)
    o_inter = lax.dot_general(q_in, st.astype(BF16), _NT, preferred_element_type=F32)
    b_last = b_ref[pl.ds(b_row0 + CHUNK - 1, 1), cols]
    k_up = (kk * jnp.exp2(b_last - b)).astype(BF16)
    st_ref[...] = st * jnp.exp2(b_last) + lax.dot_general(v, k_up, _TN, preferred_element_type=F32)

    o = o_intra + o_inter
    o = o * lax.rsqrt(jnp.mean(o * o, axis=-1, keepdims=True) + RMS_EPS) * gn
    return (o * (ogr * _sigmoid(ogr))).astype(BF16)


def _conv_rows_group(ubuf_ref, r0, cols, w_ref, cb, ng, nb):
    base = r0 + (CONV_HALO - (CONV_KERNEL - 1))
    acc = jnp.broadcast_to(cb, (CONV_ROWS, LANES))
    for r in range(SUBLANES):
        taps = list(range(r, CONV_KERNEL, SUBLANES))
        xr = ubuf_ref[pl.ds(base + r, CONV_ROWS + (len(taps) - 1) * SUBLANES), :]
        for a, k in enumerate(taps):
            acc = acc + w_ref[pl.ds(k, 1), cols] * xr[a * SUBLANES:a * SUBLANES + CONV_ROWS, :]
    y = _layer_norm(acc, ng, nb, LN_EPS)
    return (y * _sigmoid(y)).astype(BF16)


def _mixer_kernel(n_tiles, tiles_per_seq,
                  a_ref, gate_ref, q_ref, f_ref, i_ref, og_ref, cw_ref, ccb_ref, cng_ref, cnb_ref,
                  lbl_ref, gn_ref, msk_ref, tri_ref, eye_ref, wu_ref, wd_ref,
                  x_ref, mu_ref, rs_ref, g0_ref, b0_ref, wout_ref, g1_ref, b1_ref,
                  h_ref, hb_ref, wub_ref, wdb_ref,
                  st_ref, b_ref, fg_ref, ubuf_ref, uo_ref, mix_ref):
    tt = TT_HGRN
    s = pl.program_id(0)
    seq_start = (jnp.minimum(s, n_tiles - 1) % tiles_per_seq) == 0
    slot_w = s % 2
    slot_r = 1 - slot_w

    @pl.when(s == 0)
    def _():
        uo_ref[...] = jnp.zeros(uo_ref.shape, BF16)

    @pl.when(seq_start)
    def _():
        st_ref[...] = jnp.zeros((HGRN_HEADS, HEAD_DIM, HEAD_DIM), F32)
        ubuf_ref[:, pl.ds(0, CONV_HALO), :] = jnp.zeros((CONV_GROUPS, CONV_HALO, LANES), F32)

    @pl.when(jnp.logical_not(seq_start))
    def _():
        ubuf_ref[:, pl.ds(0, CONV_HALO), :] = ubuf_ref[:, pl.ds(tt, CONV_HALO), :]

    for f in range(D_FF // TF_FFN):
        wub_ref[:, pl.ds(2 * f * TF_FFN, TF_FFN)] = wu_ref[:, pl.ds(f * TF_FFN, TF_FFN)].astype(BF16)
        wub_ref[:, pl.ds((2 * f + 1) * TF_FFN, TF_FFN)] = (
            wu_ref[:, pl.ds(D_FF + f * TF_FFN, TF_FFN)].astype(BF16))
    wdb_ref[...] = wd_ref[...].astype(BF16)

    rows = [lbl_ref[pl.ds(r, 1), :] for r in range(DEPTH + 1)]
    mx = functools.reduce(jnp.maximum, rows)
    ex = [jnp.exp(r - mx) for r in rows]
    lb_all = ex[0] / functools.reduce(lambda a, c: a + c, ex)
    gn_all = gn_ref[...]

    tri = tri_ref[...]
    for rb in range(tt // CUM_ROWS):
        crows = pl.ds(rb * CUM_ROWS, CUM_ROWS)
        fg = lb_all + (1.0 - lb_all) * _sigmoid(f_ref[crows, :].astype(F32))
        g2 = jnp.log(fg) * LOG2E
        g_hi = g2.astype(BF16)
        g_lo = (g2 - g_hi.astype(F32)).astype(BF16)
        b_ref[crows, :] = (jnp.dot(tri, g_hi, preferred_element_type=F32)
                           + jnp.dot(tri, g_lo, preferred_element_type=F32))
        fg_ref[crows, :] = fg

    sub = lax.broadcasted_iota(jnp.int32, (SUBLANES, LANES), 0)
    right_lo = [(sub % (2 << l)) >= (1 << l) for l in range(3)]
    sgn_lo = [jnp.where(r, 1.0, -1.0).astype(F32) for r in right_lo]
    eye = eye_ref[...]
    ccb_all = ccb_ref[...]
    cng_all = cng_ref[...]
    cnb_all = cnb_ref[...]

    n_units = (tt // CHUNK) * HGRN_HEADS
    out_cols = D_MODEL // OUT_PIECES
    n_ln = tt // LN_ROWS_INLINE
    ln_per_unit = -(-n_ln // (n_units - OUT_PIECES))

    for c in range(tt // CHUNK):
        r0 = c * CHUNK
        rws = pl.ds(r0, CHUNK)
        for h in range(HGRN_HEADS):
            cols = pl.ds(h * HEAD_DIM, HEAD_DIM)
            lane = slice(h * HEAD_DIM, (h + 1) * HEAD_DIM)
            ubuf_ref[h, pl.ds(CONV_HALO + r0, CHUNK), :] = (
                a_ref[rws, cols].astype(F32) * _sigmoid(gate_ref[rws, cols].astype(F32)))
            uo_ref[slot_w, rws, pl.ds(CONV_WIDTH + h * HEAD_DIM, HEAD_DIM)] = _hgrn_chunk_head(
                q_ref[rws, cols].astype(F32), i_ref[rws, cols], og_ref[rws, cols].astype(F32),
                fg_ref[rws, cols], b_ref[rws, cols], b_ref, r0, cols, gn_all[:, lane], st_ref.at[h],
                msk_ref, eye, sub, right_lo, sgn_lo)
            for j in range(CHUNK // CONV_ROWS):
                rr = r0 + j * CONV_ROWS
                uo_ref[slot_w, pl.ds(rr, CONV_ROWS), cols] = _conv_rows_group(
                    ubuf_ref.at[h], rr, cols, cw_ref, ccb_all[:, lane], cng_all[:, lane],
                    cnb_all[:, lane])

            unit = c * HGRN_HEADS + h
            if unit < OUT_PIECES:
                pc = pl.ds(unit * out_cols, out_cols)
                mix_ref[:, pc] = jnp.dot(uo_ref[slot_r], wout_ref[:, pc], preferred_element_type=F32)
            else:
                for r in range((unit - OUT_PIECES) * ln_per_unit,
                               min((unit - OUT_PIECES + 1) * ln_per_unit, n_ln)):
                    rows = pl.ds(r * LN_ROWS_INLINE, LN_ROWS_INLINE)
                    rep = D_MODEL // LANES
                    mu = jnp.tile(mu_ref[rows, :], (1, rep))
                    rs = jnp.tile(rs_ref[rows, :], (1, rep))
                    h0 = (x_ref[rows, :] - mu) * rs * g0_ref[...] + b0_ref[...]
                    y = ALPHA * h0 + mix_ref[rows, :]
                    h1 = _layer_norm(y, g1_ref[...], b1_ref[...], LN_EPS)
                    h_ref[rows, :] = h1
                    hb_ref[rows, :] = h1.astype(BF16)


def _mixer(proj, conv_w, conv_b, conv_ng, conv_nb, lb_logits, gn, w_up, w_down,
           x2, mu, rs, g0, b0, w_out_bf16, g1, b1, batch, seq):
    assert CONV_GROUPS == HGRN_HEADS and CONV_WIDTH == HGRN_WIDTH
    m = proj.shape[0]
    nt = seq // TT_HGRN
    n_steps = batch * nt
    msk, tri, eye = _hgrn_constants()
    col0 = 2 * CONV_WIDTH // HGRN_WIDTH
    up_rows = w_up.shape[0] // n_steps
    down_rows = w_down.shape[0] // n_steps
    assert up_rows * n_steps == w_up.shape[0] and down_rows * n_steps == w_down.shape[0]
    assert up_rows % 16 == 0 and down_rows % 16 == 0
    cur = lambda s: jnp.minimum(s, n_steps - 1)
    prev = lambda s: jnp.maximum(s - 1, 0)

    def col(n):
        return lambda s: (cur(s), n)

    slab = lambda s: (cur(s), 0)
    prow = lambda s: (prev(s), 0)
    const2 = lambda s: (0, 0)
    return pl.pallas_call(
        functools.partial(_mixer_kernel, n_steps, nt),
        grid=(n_steps + 1,),
        in_specs=[
            pl.BlockSpec((TT_HGRN, CONV_WIDTH), col(0)),
            pl.BlockSpec((TT_HGRN, CONV_WIDTH), col(1)),
            pl.BlockSpec((TT_HGRN, HGRN_WIDTH), col(col0)),
            pl.BlockSpec((TT_HGRN, HGRN_WIDTH), col(col0 + 1)),
            pl.BlockSpec((TT_HGRN, HGRN_WIDTH), col(col0 + 2)),
            pl.BlockSpec((TT_HGRN, HGRN_WIDTH), col(col0 + 3)),
            pl.BlockSpec((CONV_KERNEL, CONV_WIDTH), const2),
            pl.BlockSpec((1, CONV_WIDTH), const2),
            pl.BlockSpec((1, CONV_WIDTH), const2),
            pl.BlockSpec((1, CONV_WIDTH), const2),
            pl.BlockSpec((DEPTH + 1, HGRN_WIDTH), const2),
            pl.BlockSpec((1, HGRN_WIDTH), const2),
            pl.BlockSpec((N_LEVELS, CHUNK, CHUNK), lambda s: (0, 0, 0)),
            pl.BlockSpec((CUM_ROWS, CUM_ROWS), const2),
            pl.BlockSpec((CHUNK, CHUNK), const2),
            pl.BlockSpec((up_rows, w_up.shape[1]), slab),
            pl.BlockSpec((down_rows, w_down.shape[1]), slab),
            pl.BlockSpec((TT_HGRN, D_MODEL), prow),
            pl.BlockSpec((TT_HGRN, LANES), prow),
            pl.BlockSpec((TT_HGRN, LANES), prow),
            pl.BlockSpec((1, D_MODEL), const2),
            pl.BlockSpec((1, D_MODEL), const2),
            pl.BlockSpec((D_MODEL, D_MODEL), const2, pipeline_mode=pl.Buffered(1)),
            pl.BlockSpec((1, D_MODEL), const2),
            pl.BlockSpec((1, D_MODEL), const2),
        ],
        out_specs=[
            pl.BlockSpec((TT_HGRN, D_MODEL), prow),
            pl.BlockSpec((TT_HGRN, D_MODEL), prow),
            pl.BlockSpec((up_rows, w_up.shape[1]), slab),
            pl.BlockSpec((down_rows, w_down.shape[1]), slab),
        ],
        out_shape=[
            jax.ShapeDtypeStruct((m, D_MODEL), F32),
            jax.ShapeDtypeStruct((m, D_MODEL), BF16),
            jax.ShapeDtypeStruct(w_up.shape, BF16),
            jax.ShapeDtypeStruct(w_down.shape, BF16),
        ],
        scratch_shapes=[pltpu.VMEM((HGRN_HEADS, HEAD_DIM, HEAD_DIM), F32),
                        pltpu.VMEM((TT_HGRN, HGRN_WIDTH), F32),
                        pltpu.VMEM((TT_HGRN, HGRN_WIDTH), F32),
                        pltpu.VMEM((CONV_GROUPS, CONV_HALO + TT_HGRN, LANES), F32),
                        pltpu.VMEM((2, TT_HGRN, D_MODEL), BF16),
                        pltpu.VMEM((TT_HGRN, D_MODEL), F32)],
        compiler_params=pltpu.CompilerParams(
            dimension_semantics=("arbitrary",),
            vmem_limit_bytes=VMEM_LIMIT),
        name="mixer",
    )(proj, proj, proj, proj, proj, proj, conv_w, conv_b, conv_ng, conv_nb, lb_logits.astype(F32), gn,
      jnp.asarray(msk), jnp.asarray(tri, dtype=BF16), jnp.asarray(eye), w_up, w_down,
      x2, mu, rs, g0, b0, w_out_bf16, g1, b1)


def _ffn_kernel(tiles_per_seq, nf, hb_ref, hres_ref, wgv_ref, cw_ref, cb_ref, wd_ref, g2_ref,
                b2_ref, o_ref, acc_ref, g_ref, v_ref, tail_ref):
    s = pl.program_id(0)
    tm = TM_FFN
    n_slab = TF_FFN // LANES
    sb = jnp.maximum(s - 1, 0)
    i_b = sb // nf
    f_b = sb % nf

    @pl.when(s == 0)
    def _():
        acc_ref[...] = jnp.zeros(acc_ref.shape, F32)
        g_ref[...] = jnp.zeros(g_ref.shape, F32)
        v_ref[...] = jnp.zeros(v_ref.shape, F32)
        tail_ref[...] = jnp.zeros(tail_ref.shape, F32)

    def step(slot_a, slot_b):
        gv = jnp.dot(hb_ref[...], wgv_ref[...], preferred_element_type=F32)
        for j in range(n_slab):
            g_ref[slot_a, j, pl.ds(SUBLANES, tm), :] = gv[:, j * LANES:(j + 1) * LANES]
        v_ref[slot_a] = gv[:, TF_FFN:]

        seq_start = (i_b % tiles_per_seq) == 0
        g_ref[slot_b, :, pl.ds(0, SUBLANES), :] = jnp.where(seq_start, 0.0, tail_ref[f_b])
        tail_ref[f_b] = g_ref[slot_b, :, pl.ds(tm, SUBLANES), :]
        part = None
        slabs_per_k = FFN_K_COLS // LANES
        for kc in range(TF_FFN // FFN_K_COLS):
            acts = []
            for j in range(kc * slabs_per_k, (kc + 1) * slabs_per_k):
                cols = pl.ds(j * LANES, LANES)
                conv = (cb_ref[:, cols]
                        + cw_ref[pl.ds(2, 1), cols] * g_ref[slot_b, j, pl.ds(SUBLANES, tm), :]
                        + cw_ref[pl.ds(1, 1), cols] * g_ref[slot_b, j, pl.ds(SUBLANES - 1, tm), :]
                        + cw_ref[pl.ds(0, 1), cols] * g_ref[slot_b, j, pl.ds(SUBLANES - 2, tm), :])
                acts.append((conv * _sigmoid(conv) * v_ref[slot_b, :, cols]).astype(BF16))
            act = jnp.concatenate(acts, axis=1)
            d = jnp.dot(act, wd_ref[pl.ds(kc * FFN_K_COLS, FFN_K_COLS), :], preferred_element_type=F32)
            part = d if part is None else part + d
        acc_ref[...] += part

    @pl.when(s % 2 == 0)
    def _():
        step(0, 1)

    @pl.when(s % 2 == 1)
    def _():
        step(1, 0)

    @pl.when(jnp.logical_and(s > 0, f_b == nf - 1))
    def _():
        g2 = g2_ref[...]
        b2 = b2_ref[...]
        for r in range(tm // LN_ROWS_INLINE):
            rows = pl.ds(r * LN_ROWS_INLINE, LN_ROWS_INLINE)
            y = ALPHA * hres_ref[rows, :] + acc_ref[rows, :]
            o_ref[rows, :] = _layer_norm(y, g2, b2, LN_EPS)
            acc_ref[rows, :] = jnp.zeros((LN_ROWS_INLINE, D_MODEL), F32)


def _ffn(h1, h1_bf16, w_up_bf16, cw, cb, w_down_bf16, g2, b2, seq):
    m = h1.shape[0]
    nf = D_FF // TF_FFN
    n_steps = (m // TM_FFN) * nf
    a_step = lambda s: jnp.minimum(s, n_steps - 1)
    b_step = lambda s: jnp.maximum(s - 1, 0)
    return pl.pallas_call(
        functools.partial(_ffn_kernel, seq // TM_FFN, nf),
        grid=(n_steps + 1,),
        in_specs=[
            pl.BlockSpec((TM_FFN, D_MODEL), lambda s: (a_step(s) // nf, 0)),
            pl.BlockSpec((TM_FFN, D_MODEL), lambda s: (b_step(s) // nf, 0)),
            pl.BlockSpec((D_MODEL, 2 * TF_FFN), lambda s: (0, a_step(s) % nf)),
            pl.BlockSpec((FFN_KERNEL, TF_FFN), lambda s: (0, b_step(s) % nf)),
            pl.BlockSpec((1, TF_FFN), lambda s: (0, b_step(s) % nf)),
            pl.BlockSpec((TF_FFN, D_MODEL), lambda s: (b_step(s) % nf, 0)),
            pl.BlockSpec((1, D_MODEL), lambda s: (0, 0)),
            pl.BlockSpec((1, D_MODEL), lambda s: (0, 0)),
        ],
        out_specs=pl.BlockSpec((TM_FFN, D_MODEL), lambda s: (b_step(s) // nf, 0)),
        out_shape=jax.ShapeDtypeStruct((m, D_MODEL), F32),
        scratch_shapes=[
            pltpu.VMEM((TM_FFN, D_MODEL), F32),
            pltpu.VMEM((2, TF_FFN // LANES, SUBLANES + TM_FFN, LANES), F32),
            pltpu.VMEM((2, TM_FFN, TF_FFN), F32),
            pltpu.VMEM((nf, TF_FFN // LANES, SUBLANES, LANES), F32),
        ],
        compiler_params=pltpu.CompilerParams(
            dimension_semantics=("arbitrary",),
            vmem_limit_bytes=VMEM_LIMIT),
        name="conv_ffn_ln2",
    )(h1_bf16, h1, w_up_bf16, cw, cb, w_down_bf16, g2, b2)


def kernel(x, emb_ln_g, emb_ln_b, w_in, conv_w, conv_b, conv_norm_g, conv_norm_b, lb_logits,
           hgrn_norm_g, w_out, ln1_g, ln1_b, w_ffn_up, ffn_conv_w, ffn_conv_b, w_ffn_down,
           ln2_g, ln2_b):
    batch, seq, d = x.shape
    assert d == D_MODEL and w_in.shape[0] == DEPTH == 1
    assert seq % TT_HGRN == 0 and seq % TM_FFN == 0
    x2 = x.reshape(batch * seq, d)
    row = lambda a: a.reshape(1, -1).astype(F32)

    proj, mu0, rs0 = _inproj(x2, row(emb_ln_g), row(emb_ln_b), w_in[0])
    h1, h1_bf16, w_up_bf16, w_down_bf16 = _mixer(
        proj, conv_w[0], row(conv_b[0]), row(conv_norm_g[0]), row(conv_norm_b[0]), lb_logits,
        row(hgrn_norm_g[0]), w_ffn_up[0], w_ffn_down[0],
        x2, mu0, rs0, row(emb_ln_g), row(emb_ln_b), w_out[0].astype(BF16), row(ln1_g[0]),
        row(ln1_b[0]), batch, seq)
    out = _ffn(h1, h1_bf16, w_up_bf16, ffn_conv_w[0], row(ffn_conv_b[0]),
               w_down_bf16, row(ln2_g[0]), row(ln2_b[0]), seq)
    return out.reshape(batch, seq, d)
```

```python
import functools

import jax
import jax.numpy as jnp
import numpy as np
from jax import lax
from jax.experimental import pallas as pl
from jax.experimental.pallas import tpu as pltpu

F32 = jnp.float32
BF16 = jnp.bfloat16

D_MODEL = 2048
CONV_WIDTH = 1024
CONV_GROUPS = 8
CONV_KERNEL = 31
HGRN_WIDTH = 1024
HGRN_HEADS = 8
HEAD_DIM = 128
IN_PROJ_DIM = 2 * CONV_WIDTH + 4 * HGRN_WIDTH
D_FF = 5632
FFN_KERNEL = 3
LN_EPS = 1e-5
RMS_EPS = 1e-6
DEPTH = 1
ALPHA = (2.0 * DEPTH) ** 0.25
LOG2E = 1.4426950408889634

LANES = 128
SUBLANES = 8
VMEM_LIMIT = 56 * 1024 * 1024

TM_IN, TN_IN = 1024, 1024
IN_PIECE_ROWS = 256
CONV_HALO = 32
TT_HGRN = 256
OUT_PIECES = 8
CHUNK = 128
N_LEVELS = 7
CUM_ROWS = 256
CONV_ROWS = 64
N_SLABS = CHUNK // SUBLANES
TM_FFN, TF_FFN = 512, 512
FFN_K_COLS = 256
LN_ROWS_INLINE = 16

_NT = (((1,), (1,)), ((), ()))
_TN = (((0,), (0,)), ((), ()))


def _ln_stats(x, eps):
    mu = jnp.mean(x, axis=-1, keepdims=True)
    xc = x - mu
    var = jnp.mean(xc * xc, axis=-1, keepdims=True)
    return mu, lax.rsqrt(var + eps), xc


def _layer_norm(x, g, b, eps):
    _, rs, xc = _ln_stats(x, eps)
    return xc * rs * g + b


def _sigmoid(x):
    return 1.0 / (1.0 + jnp.exp2(x * (-LOG2E)))


def _inproj_kernel(x_ref, g_ref, b_ref, w_ref, o_ref, mu_ref, rs_ref, xn_ref):
    j = pl.program_id(1)
    hf = pl.program_id(2)

    def first_column_tile(slot):
        g = g_ref[...]
        b = b_ref[...]
        wb = w_ref[...].astype(BF16)
        for p in range(TM_IN // IN_PIECE_ROWS):
            for r in range(IN_PIECE_ROWS // LN_ROWS_INLINE):
                rows = pl.ds(p * IN_PIECE_ROWS + r * LN_ROWS_INLINE, LN_ROWS_INLINE)
                mu, rs, xc = _ln_stats(x_ref[rows, :], LN_EPS)
                mu_ref[rows, :] = jnp.broadcast_to(mu, (LN_ROWS_INLINE, LANES))
                rs_ref[rows, :] = jnp.broadcast_to(rs, (LN_ROWS_INLINE, LANES))
                xn_ref[slot, rows, :] = (xc * rs * g + b).astype(BF16)
            prows = pl.ds(p * IN_PIECE_ROWS, IN_PIECE_ROWS)
            o_ref[prows, :] = jnp.dot(xn_ref[slot, prows, :], wb,
                                      preferred_element_type=F32).astype(BF16)

    def other_column_tile(slot):
        o_ref[...] = jnp.dot(xn_ref[slot], w_ref[...].astype(BF16),
                             preferred_element_type=F32).astype(BF16)

    for slot in range(2):
        pl.when(jnp.logical_and(j == 0, hf == slot))(functools.partial(first_column_tile, slot))
        pl.when(jnp.logical_and(j != 0, hf == slot))(functools.partial(other_column_tile, slot))


def _inproj(x2, g, b, w_f32):
    m = x2.shape[0]
    xrow = lambda i2, j, hf: (jnp.where(j == 0, 2 * i2 + hf, 2 * i2 + 1), 0)
    const = lambda i2, j, hf: (0, 0)
    return pl.pallas_call(
        _inproj_kernel,
        grid=(m // (2 * TM_IN), IN_PROJ_DIM // TN_IN, 2),
        in_specs=[
            pl.BlockSpec((TM_IN, D_MODEL), xrow),
            pl.BlockSpec((1, D_MODEL), const),
            pl.BlockSpec((1, D_MODEL), const),
            pl.BlockSpec((D_MODEL, TN_IN), lambda i2, j, hf: (0, j)),
        ],
        out_specs=[pl.BlockSpec((TM_IN, TN_IN), lambda i2, j, hf: (2 * i2 + hf, j)),
                   pl.BlockSpec((TM_IN, LANES), xrow),
                   pl.BlockSpec((TM_IN, LANES), xrow)],
        out_shape=[jax.ShapeDtypeStruct((m, IN_PROJ_DIM), BF16),
                   jax.ShapeDtypeStruct((m, LANES), F32),
                   jax.ShapeDtypeStruct((m, LANES), F32)],
        scratch_shapes=[pltpu.VMEM((2, TM_IN, D_MODEL), BF16)],
        compiler_params=pltpu.CompilerParams(
            dimension_semantics=("arbitrary", "arbitrary", "arbitrary"),
            vmem_limit_bytes=VMEM_LIMIT),
        name="ln_inproj",
    )(x2, g, b, w_f32)


def _hgrn_constants():
    t = np.arange(CHUNK)
    msk = np.zeros((N_LEVELS, CHUNK, CHUNK), np.float32)
    for l in range(N_LEVELS):
        m = 1 << l
        right = (t % (2 * m)) >= m
        same = (t[:, None] // (2 * m)) == (t[None, :] // (2 * m))
        msk[l] = (same & right[:, None] & (~right)[None, :]).astype(np.float32)
    r = np.arange(CUM_ROWS)
    tri = ((r[:, None] >= r[None, :]) & (r[:, None] // CHUNK == r[None, :] // CHUNK)).astype(np.float32)
    eye = np.eye(CHUNK, dtype=np.float32)
    return msk, tri, eye


def _hgrn_chunk_head(qr, vr, ogr, fg, b, b_ref, b_row0, cols, gn, st_ref, msk_ref, eye, sub, right_lo,
                     sgn_lo):
    kk = 1.0 - fg
    qh = qr * _sigmoid(qr)
    v = vr

    def slabs(x):
        return [x[i * SUBLANES:(i + 1) * SUBLANES] for i in range(N_SLABS)]

    def row(r):
        return jnp.broadcast_to(b_ref[pl.ds(b_row0 + r, 1), cols], (SUBLANES, LANES))

    bs, qs, ks, fs = slabs(b), slabs(qh), slabs(kk), slabs(fg)
    sc = slabs(eye * jnp.sum(qh * kk, axis=-1, keepdims=True))
    for l in range(N_LEVELS):
        m = 1 << l
        zs = []
        for i in range(N_SLABS):
            base = i * SUBLANES
            if m >= SUBLANES:
                bnd = row(base - base % (2 * m) + m - 1)
                if base % (2 * m) >= m:
                    zs.append(qs[i] * jnp.exp2(bs[i] - bnd))
                else:
                    zs.append(ks[i] * jnp.exp2(bnd - bs[i]))
            elif l == 0:
                zs.append(jnp.where(right_lo[0], qs[i] * fs[i], ks[i]))
            else:
                bnd = row(base + m - 1)
                for pp in range(1, SUBLANES // (2 * m)):
                    bnd = jnp.where(sub >= pp * 2 * m, row(base + pp * 2 * m + m - 1), bnd)
                e = (bs[i] - bnd) * sgn_lo[l]
                zs.append(jnp.where(right_lo[l], qs[i], ks[i]) * jnp.exp2(e))
        z = jnp.concatenate(zs, axis=0)
        zb = z.astype(BF16)
        s_l = jnp.dot(zb, zb.T, preferred_element_type=F32)
        for i in range(N_SLABS):
            base = i * SUBLANES
            if m >= SUBLANES and base % (2 * m) < m:
                continue
            sc[i] = sc[i] + msk_ref[l, pl.ds(base, SUBLANES), :] * s_l[base:base + SUBLANES]
    scores = jnp.concatenate(sc, axis=0)
    o_intra = jnp.dot(scores.astype(BF16), v, preferred_element_type=F32)

    st = st_ref[...]
    q_in = (qh * jnp.exp2(b)).astype(BF16)
    o_inter = lax.dot_general(q_in, st.astype(BF16), _NT, preferred_element_type=F32)
    b_last = b_ref[pl.ds(b_row0 + CHUNK - 1, 1), cols]
    k_up = (kk * jnp.exp2(b_last - b)).astype(BF16)
    st_ref[...] = st * jnp.exp2(b_last) + lax.dot_general(v, k_up, _TN, preferred_element_type=F32)

    o = o_intra + o_inter
    o = o * lax.rsqrt(jnp.mean(o * o, axis=-1, keepdims=True) + RMS_EPS) * gn
    return (o * (ogr * _sigmoid(ogr))).astype(BF16)


def _conv_rows_group(ubuf_ref, r0, cols, w_ref, cb, ng, nb):
    base = r0 + (CONV_HALO - (CONV_KERNEL - 1))
    acc = jnp.broadcast_to(cb, (CONV_ROWS, LANES))
    for r in range(SUBLANES):
        taps = list(range(r, CONV_KERNEL, SUBLANES))
        xr = ubuf_ref[pl.ds(base + r, CONV_ROWS + (len(taps) - 1) * SUBLANES), :]
        for a, k in enumerate(taps):
            acc = acc + w_ref[pl.ds(k, 1), cols] * xr[a * SUBLANES:a * SUBLANES + CONV_ROWS, :]
    y = _layer_norm(acc, ng, nb, LN_EPS)
    return (y * _sigmoid(y)).astype(BF16)


def _mixer_kernel(n_tiles, tiles_per_seq,
                  a_ref, gate_ref, q_ref, f_ref, i_ref, og_ref, cw_ref, ccb_ref, cng_ref, cnb_ref,
                  lbl_ref, gn_ref, msk_ref, tri_ref, eye_ref, wu_ref, wd_ref,
                  x_ref, mu_ref, rs_ref, g0_ref, b0_ref, wout_ref, g1_ref, b1_ref,
                  h_ref, hb_ref, wub_ref, wdb_ref,
                  st_ref, b_ref, fg_ref, ubuf_ref, uo_ref, mix_ref):
    tt = TT_HGRN
    s = pl.program_id(0)
    seq_start = (jnp.minimum(s, n_tiles - 1) % tiles_per_seq) == 0
    slot_w = s % 2
    slot_r = 1 - slot_w

    @pl.when(s == 0)
    def _():
        uo_ref[...] = jnp.zeros(uo_ref.shape, BF16)

    @pl.when(seq_start)
    def _():
        st_ref[...] = jnp.zeros((HGRN_HEADS, HEAD_DIM, HEAD_DIM), F32)
        ubuf_ref[:, pl.ds(0, CONV_HALO), :] = jnp.zeros((CONV_GROUPS, CONV_HALO, LANES), F32)

    @pl.when(jnp.logical_not(seq_start))
    def _():
        ubuf_ref[:, pl.ds(0, CONV_HALO), :] = ubuf_ref[:, pl.ds(tt, CONV_HALO), :]

    for f in range(D_FF // TF_FFN):
        wub_ref[:, pl.ds(2 * f * TF_FFN, TF_FFN)] = wu_ref[:, pl.ds(f * TF_FFN, TF_FFN)].astype(BF16)
        wub_ref[:, pl.ds((2 * f + 1) * TF_FFN, TF_FFN)] = (
            wu_ref[:, pl.ds(D_FF + f * TF_FFN, TF_FFN)].astype(BF16))
    wdb_ref[...] = wd_ref[...].astype(BF16)

    rows = [lbl_ref[pl.ds(r, 1), :] for r in range(DEPTH + 1)]
    mx = functools.reduce(jnp.maximum, rows)
    ex = [jnp.exp(r - mx) for r in rows]
    lb_all = ex[0] / functools.reduce(lambda a, c: a + c, ex)
    gn_all = gn_ref[...]

    tri = tri_ref[...]
    for rb in range(tt // CUM_ROWS):
        crows = pl.ds(rb * CUM_ROWS, CUM_ROWS)
        fg = lb_all + (1.0 - lb_all) * _sigmoid(f_ref[crows, :].astype(F32))
        g2 = jnp.log(fg) * LOG2E
        g_hi = g2.astype(BF16)
        g_lo = (g2 - g_hi.astype(F32)).astype(BF16)
        b_ref[crows, :] = (jnp.dot(tri, g_hi, preferred_element_type=F32)
                           + jnp.dot(tri, g_lo, preferred_element_type=F32))
        fg_ref[crows, :] = fg

    sub = lax.broadcasted_iota(jnp.int32, (SUBLANES, LANES), 0)
    right_lo = [(sub % (2 << l)) >= (1 << l) for l in range(3)]
    sgn_lo = [jnp.where(r, 1.0, -1.0).astype(F32) for r in right_lo]
    eye = eye_ref[...]
    ccb_all = ccb_ref[...]
    cng_all = cng_ref[...]
    cnb_all = cnb_ref[...]

    n_units = (tt // CHUNK) * HGRN_HEADS
    out_cols = D_MODEL // OUT_PIECES
    n_ln = tt // LN_ROWS_INLINE
    ln_per_unit = -(-n_ln // (n_units - OUT_PIECES))

    for c in range(tt // CHUNK):
        r0 = c * CHUNK
        rws = pl.ds(r0, CHUNK)
        for h in range(HGRN_HEADS):
            cols = pl.ds(h * HEAD_DIM, HEAD_DIM)
            lane = slice(h * HEAD_DIM, (h + 1) * HEAD_DIM)
            ubuf_ref[h, pl.ds(CONV_HALO + r0, CHUNK), :] = (
                a_ref[rws, cols].astype(F32) * _sigmoid(gate_ref[rws, cols].astype(F32)))
            uo_ref[slot_w, rws, pl.ds(CONV_WIDTH + h * HEAD_DIM, HEAD_DIM)] = _hgrn_chunk_head(
                q_ref[rws, cols].astype(F32), i_ref[rws, cols], og_ref[rws, cols].astype(F32),
                fg_ref[rws, cols], b_ref[rws, cols], b_ref, r0, cols, gn_all[:, lane], st_ref.at[h],
                msk_ref, eye, sub, right_lo, sgn_lo)
            for j in range(CHUNK // CONV_ROWS):
                rr = r0 + j * CONV_ROWS
                uo_ref[slot_w, pl.ds(rr, CONV_ROWS), cols] = _conv_rows_group(
                    ubuf_ref.at[h], rr, cols, cw_ref, ccb_all[:, lane], cng_all[:, lane],
                    cnb_all[:, lane])

            unit = c * HGRN_HEADS + h
            if unit < OUT_PIECES:
                pc = pl.ds(unit * out_cols, out_cols)
                mix_ref[:, pc] = jnp.dot(uo_ref[slot_r], wout_ref[:, pc], preferred_element_type=F32)
            else:
                for r in range((unit - OUT_PIECES) * ln_per_unit,
                               min((unit - OUT_PIECES + 1) * ln_per_unit, n_ln)):
                    rows = pl.ds(r * LN_ROWS_INLINE, LN_ROWS_INLINE)
                    rep = D_MODEL // LANES
                    mu = jnp.tile(mu_ref[rows, :], (1, rep))
                    rs = jnp.tile(rs_ref[rows, :], (1, rep))
                    h0 = (x_ref[rows, :] - mu) * rs * g0_ref[...] + b0_ref[...]
                    y = ALPHA * h0 + mix_ref[rows, :]
                    h1 = _layer_norm(y, g1_ref[...], b1_ref[...], LN_EPS)
                    h_ref[rows, :] = h1
                    hb_ref[rows, :] = h1.astype(BF16)


def _mixer(proj, conv_w, conv_b, conv_ng, conv_nb, lb_logits, gn, w_up, w_down,
           x2, mu, rs, g0, b0, w_out_bf16, g1, b1, batch, seq):
    assert CONV_GROUPS == HGRN_HEADS and CONV_WIDTH == HGRN_WIDTH
    m = proj.shape[0]
    nt = seq // TT_HGRN
    n_steps = batch * nt
    msk, tri, eye = _hgrn_constants()
    col0 = 2 * CONV_WIDTH // HGRN_WIDTH
    up_rows = w_up.shape[0] // n_steps
    down_rows = w_down.shape[0] // n_steps
    assert up_rows * n_steps == w_up.shape[0] and down_rows * n_steps == w_down.shape[0]
    assert up_rows % 16 == 0 and down_rows % 16 == 0
    cur = lambda s: jnp.minimum(s, n_steps - 1)
    prev = lambda s: jnp.maximum(s - 1, 0)

    def col(n):
        return lambda s: (cur(s), n)

    slab = lambda s: (cur(s), 0)
    prow = lambda s: (prev(s), 0)
    const2 = lambda s: (0, 0)
    return pl.pallas_call(
        functools.partial(_mixer_kernel, n_steps, nt),
        grid=(n_steps + 1,),
        in_specs=[
            pl.BlockSpec((TT_HGRN, CONV_WIDTH), col(0)),
            pl.BlockSpec((TT_HGRN, CONV_WIDTH), col(1)),
            pl.BlockSpec((TT_HGRN, HGRN_WIDTH), col(col0)),
            pl.BlockSpec((TT_HGRN, HGRN_WIDTH), col(col0 + 1)),
            pl.BlockSpec((TT_HGRN, HGRN_WIDTH), col(col0 + 2)),
            pl.BlockSpec((TT_HGRN, HGRN_WIDTH), col(col0 + 3)),
            pl.BlockSpec((CONV_KERNEL, CONV_WIDTH), const2),
            pl.BlockSpec((1, CONV_WIDTH), const2),
            pl.BlockSpec((1, CONV_WIDTH), const2),
            pl.BlockSpec((1, CONV_WIDTH), const2),
            pl.BlockSpec((DEPTH + 1, HGRN_WIDTH), const2),
            pl.BlockSpec((1, HGRN_WIDTH), const2),
            pl.BlockSpec((N_LEVELS, CHUNK, CHUNK), lambda s: (0, 0, 0)),
            pl.BlockSpec((CUM_ROWS, CUM_ROWS), const2),
            pl.BlockSpec((CHUNK, CHUNK), const2),
            pl.BlockSpec((up_rows, w_up.shape[1]), slab),
            pl.BlockSpec((down_rows, w_down.shape[1]), slab),
            pl.BlockSpec((TT_HGRN, D_MODEL), prow),
            pl.BlockSpec((TT_HGRN, LANES), prow),
            pl.BlockSpec((TT_HGRN, LANES), prow),
            pl.BlockSpec((1, D_MODEL), const2),
            pl.BlockSpec((1, D_MODEL), const2),
            pl.BlockSpec((D_MODEL, D_MODEL), const2, pipeline_mode=pl.Buffered(1)),
            pl.BlockSpec((1, D_MODEL), const2),
            pl.BlockSpec((1, D_MODEL), const2),
        ],
        out_specs=[
            pl.BlockSpec((TT_HGRN, D_MODEL), prow),
            pl.BlockSpec((TT_HGRN, D_MODEL), prow),
            pl.BlockSpec((up_rows, w_up.shape[1]), slab),
            pl.BlockSpec((down_rows, w_down.shape[1]), slab),
        ],
        out_shape=[
            jax.ShapeDtypeStruct((m, D_MODEL), F32),
            jax.ShapeDtypeStruct((m, D_MODEL), BF16),
            jax.ShapeDtypeStruct(w_up.shape, BF16),
            jax.ShapeDtypeStruct(w_down.shape, BF16),
        ],
        scratch_shapes=[pltpu.VMEM((HGRN_HEADS, HEAD_DIM, HEAD_DIM), F32),
                        pltpu.VMEM((TT_HGRN, HGRN_WIDTH), F32),
                        pltpu.VMEM((TT_HGRN, HGRN_WIDTH), F32),
                        pltpu.VMEM((CONV_GROUPS, CONV_HALO + TT_HGRN, LANES), F32),
                        pltpu.VMEM((2, TT_HGRN, D_MODEL), BF16),
                        pltpu.VMEM((TT_HGRN, D_MODEL), F32)],
        compiler_params=pltpu.CompilerParams(
            dimension_semantics=("arbitrary",),
            vmem_limit_bytes=VMEM_LIMIT),
        name="mixer",
    )(proj, proj, proj, proj, proj, proj, conv_w, conv_b, conv_ng, conv_nb, lb_logits.astype(F32), gn,
      jnp.asarray(msk), jnp.asarray(tri, dtype=BF16), jnp.asarray(eye), w_up, w_down,
      x2, mu, rs, g0, b0, w_out_bf16, g1, b1)


def _ffn_kernel(tiles_per_seq, nf, hb_ref, hres_ref, wgv_ref, cw_ref, cb_ref, wd_ref, g2_ref,
                b2_ref, o_ref, acc_ref, g_ref, v_ref, tail_ref):
    s = pl.program_id(0)
    tm = TM_FFN
    n_slab = TF_FFN // LANES
    sb = jnp.maximum(s - 1, 0)
    i_b = sb // nf
    f_b = sb % nf

    @pl.when(s == 0)
    def _():
        acc_ref[...] = jnp.zeros(acc_ref.shape, F32)
        g_ref[...] = jnp.zeros(g_ref.shape, F32)
        v_ref[...] = jnp.zeros(v_ref.shape, F32)
        tail_ref[...] = jnp.zeros(tail_ref.shape, F32)

    def step(slot_a, slot_b):
        gv = jnp.dot(hb_ref[...], wgv_ref[...], preferred_element_type=F32)
        for j in range(n_slab):
            g_ref[slot_a, j, pl.ds(SUBLANES, tm), :] = gv[:, j * LANES:(j + 1) * LANES]
        v_ref[slot_a] = gv[:, TF_FFN:]

        seq_start = (i_b % tiles_per_seq) == 0
        g_ref[slot_b, :, pl.ds(0, SUBLANES), :] = jnp.where(seq_start, 0.0, tail_ref[f_b])
        tail_ref[f_b] = g_ref[slot_b, :, pl.ds(tm, SUBLANES), :]
        part = None
        slabs_per_k = FFN_K_COLS // LANES
        for kc in range(TF_FFN // FFN_K_COLS):
            acts = []
            for j in range(kc * slabs_per_k, (kc + 1) * slabs_per_k):
                cols = pl.ds(j * LANES, LANES)
                conv = (cb_ref[:, cols]
                        + cw_ref[pl.ds(2, 1), cols] * g_ref[slot_b, j, pl.ds(SUBLANES, tm), :]
                        + cw_ref[pl.ds(1, 1), cols] * g_ref[slot_b, j, pl.ds(SUBLANES - 1, tm), :]
                        + cw_ref[pl.ds(0, 1), cols] * g_ref[slot_b, j, pl.ds(SUBLANES - 2, tm), :])
                acts.append((conv * _sigmoid(conv) * v_ref[slot_b, :, cols]).astype(BF16))
            act = jnp.concatenate(acts, axis=1)
            d = jnp.dot(act, wd_ref[pl.ds(kc * FFN_K_COLS, FFN_K_COLS), :], preferred_element_type=F32)
            part = d if part is None else part + d
        acc_ref[...] += part

    @pl.when(s % 2 == 0)
    def _():
        step(0, 1)

    @pl.when(s % 2 == 1)
    def _():
        step(1, 0)

    @pl.when(jnp.logical_and(s > 0, f_b == nf - 1))
    def _():
        g2 = g2_ref[...]
        b2 = b2_ref[...]
        for r in range(tm // LN_ROWS_INLINE):
            rows = pl.ds(r * LN_ROWS_INLINE, LN_ROWS_INLINE)
            y = ALPHA * hres_ref[rows, :] + acc_ref[rows, :]
            o_ref[rows, :] = _layer_norm(y, g2, b2, LN_EPS)
            acc_ref[rows, :] = jnp.zeros((LN_ROWS_INLINE, D_MODEL), F32)


def _ffn(h1, h1_bf16, w_up_bf16, cw, cb, w_down_bf16, g2, b2, seq):
    m = h1.shape[0]
    nf = D_FF // TF_FFN
    n_steps = (m // TM_FFN) * nf
    a_step = lambda s: jnp.minimum(s, n_steps - 1)
    b_step = lambda s: jnp.maximum(s - 1, 0)
    return pl.pallas_call(
        functools.partial(_ffn_kernel, seq // TM_FFN, nf),
        grid=(n_steps + 1,),
        in_specs=[
            pl.BlockSpec((TM_FFN, D_MODEL), lambda s: (a_step(s) // nf, 0)),
            pl.BlockSpec((TM_FFN, D_MODEL), lambda s: (b_step(s) // nf, 0)),
            pl.BlockSpec((D_MODEL, 2 * TF_FFN), lambda s: (0, a_step(s) % nf)),
            pl.BlockSpec((FFN_KERNEL, TF_FFN), lambda s: (0, b_step(s) % nf)),
            pl.BlockSpec((1, TF_FFN), lambda s: (0, b_step(s) % nf)),
            pl.BlockSpec((TF_FFN, D_MODEL), lambda s: (b_step(s) % nf, 0)),
            pl.BlockSpec((1, D_MODEL), lambda s: (0, 0)),
            pl.BlockSpec((1, D_MODEL), lambda s: (0, 0)),
        ],
        out_specs=pl.BlockSpec((TM_FFN, D_MODEL), lambda s: (b_step(s) // nf, 0)),
        out_shape=jax.ShapeDtypeStruct((m, D_MODEL), F32),
        scratch_shapes=[
            pltpu.VMEM((TM_FFN, D_MODEL), F32),
            pltpu.VMEM((2, TF_FFN // LANES, SUBLANES + TM_FFN, LANES), F32),
            pltpu.VMEM((2, TM_FFN, TF_FFN), F32),
            pltpu.VMEM((nf, TF_FFN // LANES, SUBLANES, LANES), F32),
        ],
        compiler_params=pltpu.CompilerParams(
            dimension_semantics=("arbitrary",),
            vmem_limit_bytes=VMEM_LIMIT),
        name="conv_ffn_ln2",
    )(h1_bf16, h1, w_up_bf16, cw, cb, w_down_bf16, g2, b2)


def kernel(x, emb_ln_g, emb_ln_b, w_in, conv_w, conv_b, conv_norm_g, conv_norm_b, lb_logits,
           hgrn_norm_g, w_out, ln1_g, ln1_b, w_ffn_up, ffn_conv_w, ffn_conv_b, w_ffn_down,
           ln2_g, ln2_b):
    batch, seq, d = x.shape
    assert d == D_MODEL and w_in.shape[0] == DEPTH == 1
    assert seq % TT_HGRN == 0 and seq % TM_FFN == 0
    x2 = x.reshape(batch * seq, d)
    row = lambda a: a.reshape(1, -1).astype(F32)

    proj, mu0, rs0 = _inproj(x2, row(emb_ln_g), row(emb_ln_b), w_in[0])
    h1, h1_bf16, w_up_bf16, w_down_bf16 = _mixer(
        proj, conv_w[0], row(conv_b[0]), row(conv_norm_g[0]), row(conv_norm_b[0]), lb_logits,
        row(hgrn_norm_g[0]), w_ffn_up[0], w_ffn_down[0],
        x2, mu0, rs0, row(emb_ln_g), row(emb_ln_b), w_out[0].astype(BF16), row(ln1_g[0]),
        row(ln1_b[0]), batch, seq)
    out = _ffn(h1, h1_bf16, w_up_bf16, ffn_conv_w[0], row(ffn_conv_b[0]),
               w_down_bf16, row(ln2_g[0]), row(ln2_b[0]), seq)
    return out.reshape(batch, seq, d)
```
